```python
import math
import jax, jax.numpy as jnp
from jax import lax
import numpy as np

D_MODEL = 1024
BATCH = 16
SEQ = 2048
DEPTH = 4

N_MEM = 256
HEAD_DIM = 64
N_MEM_HEADS = 4
MEM_W = N_MEM_HEADS * HEAD_DIM
MAIN_W = D_MODEL - MEM_W
MIX_W = MAIN_W + MEM_W
POOL_WINDOWS = (2, 4, 8, 16)
POOL_GROUP = MAIN_W // len(POOL_WINDOWS)
DIL_PATTERNS = ((128, 1), (512, 4), (2048, 16))
N_GROUPS = len(DIL_PATTERNS)
HEADS_PER_GROUP = MAIN_W // (N_GROUPS * HEAD_DIM)
N_DIL_HEADS = N_GROUPS * HEADS_PER_GROUP
N_A_LAYERS = DEPTH // 2
N_B_LAYERS = DEPTH - N_A_LAYERS
D_FF = ((8 * D_MODEL + 3 * 256 - 1) // (3 * 256)) * 256
ROPE_THETA = 10000.0
EPS = 1e-6
NEG = -1e30

kernel_name = "yoco_pool_dilated_hybrid"


def rms_norm(x, g):
    xf = x.astype(jnp.float32)
    y = xf * lax.rsqrt(jnp.mean(xf * xf, axis=-1, keepdims=True) + EPS)
    return (y * g.astype(jnp.float32)).astype(x.dtype)


def rope(t, positions):
    half = HEAD_DIM // 2
    freqs = ROPE_THETA ** (-jnp.arange(half, dtype=jnp.float32) / half)
    ang = positions.astype(jnp.float32)[..., None] * freqs
    cos = jnp.cos(ang)[:, :, None, :]
    sin = jnp.sin(ang)[:, :, None, :]
    tf = t.astype(jnp.float32)
    t1, t2 = tf[..., :half], tf[..., half:]
    return jnp.concatenate([t1 * cos - t2 * sin, t1 * sin + t2 * cos], axis=-1).astype(t.dtype)


def pool_mixer(u, w_pool, scale):
    S = u.shape[1]
    uf = u.astype(jnp.float32)
    c = jnp.cumsum(uf, axis=1)
    t = jnp.arange(S)
    outs = []
    for gi, w in enumerate(POOL_WINDOWS):
        sl = slice(gi * POOL_GROUP, (gi + 1) * POOL_GROUP)
        cg = c[..., sl]
        shifted = jnp.pad(cg, ((0, 0), (w, 0), (0, 0)))[:, :S]
        cnt = jnp.minimum(t + 1, w).astype(jnp.float32)[None, :, None]
        outs.append((cg - shifted) / cnt - uf[..., sl])
    p = jnp.stack(outs, axis=2).astype(u.dtype)
    y = jnp.einsum('bsgc,gce->bsge', p, w_pool).reshape(u.shape)
    return y * scale


def to_strided(t, d):
    B, S = t.shape[:2]
    return t.reshape((B, S // d, d) + t.shape[2:]).swapaxes(1, 2)


def kv_blocks(t, d, steps):
    s = to_strided(t, d)
    B, _, L, H, hd = s.shape
    nb = -(-L // steps)
    Lp = nb * steps
    s = jnp.pad(s, ((0, 0), (0, 0), (steps, Lp - L), (0, 0), (0, 0)))
    s = s.reshape(B, d, nb + 1, steps, H, hd)
    return jnp.concatenate([s[:, :, :-1], s[:, :, 1:]], axis=3)


def band_mask(nb, steps):
    i = jnp.arange(steps)[:, None] + steps
    j = jnp.arange(2 * steps)[None, :]
    rel = i - j
    m = (rel >= 0) & (rel <= steps)
    n = jnp.arange(nb)[:, None, None]
    return m[None] & ((n > 0) | (j >= steps)[None])


def dilated_group_attn(q, kk, vv, d, steps):
    B, S, H, hd = q.shape
    L = S // d
    nb = kk.shape[2]
    Lp = nb * steps
    qs = jnp.pad(to_strided(q, d), ((0, 0), (0, 0), (0, Lp - L), (0, 0), (0, 0)))
    qs = qs.reshape(B, d, nb, steps, H, hd)
    s = jnp.einsum('bdnqhe,bdnkhe->bdnhqk', qs, kk).astype(jnp.float32) * (HEAD_DIM ** -0.5)
    s = jnp.where(band_mask(nb, steps)[None, None, :, None], s, NEG)
    lse = jax.nn.logsumexp(s, axis=-1)
    p = jnp.exp(s - lse[..., None]).astype(vv.dtype)
    o = jnp.einsum('bdnhqk,bdnkhe->bdnqhe', p, vv)
    o = o.reshape(B, d, Lp, H, hd)[:, :, :L].swapaxes(1, 2).reshape(B, S, H, hd)
    lse = lse.transpose(0, 1, 2, 4, 3).reshape(B, d, Lp, H)[:, :, :L].swapaxes(1, 2).reshape(B, S, H)
    return o, lse


def build_shared_kv(x, positions, kv_norm, w_kv):
    B, S, _ = x.shape
    kv = rms_norm(x, kv_norm) @ w_kv
    k = rope(kv[..., :MAIN_W].reshape(B, S, N_DIL_HEADS, HEAD_DIM), positions)
    v = kv[..., MAIN_W:].reshape(B, S, N_DIL_HEADS, HEAD_DIM)
    k = k.reshape(B, S, N_GROUPS, HEADS_PER_GROUP, HEAD_DIM)
    v = v.reshape(B, S, N_GROUPS, HEADS_PER_GROUP, HEAD_DIM)
    shared = []
    for g, (win, d) in enumerate(DIL_PATTERNS):
        steps = win // d
        shared.append((kv_blocks(k[:, :, g], d, steps), kv_blocks(v[:, :, g], d, steps)))
    return shared


def dilated_mixer(zq, positions, shared):
    B, S, _ = zq.shape
    q = rope(zq.reshape(B, S, N_DIL_HEADS, HEAD_DIM), positions)
    q = q.reshape(B, S, N_GROUPS, HEADS_PER_GROUP, HEAD_DIM)
    outs, lses = [], []
    for g, (win, d) in enumerate(DIL_PATTERNS):
        kk, vv = shared[g]
        o, l = dilated_group_attn(q[:, :, g], kk, vv, d, win // d)
        outs.append(o)
        lses.append(l)
    o = jnp.stack(outs, axis=2)
    alpha = jax.nn.softmax(jnp.stack(lses, axis=2), axis=2)
    return (o * alpha[..., None].astype(o.dtype)).reshape(B, S, MAIN_W)


def memory_attn(zm, mem, g, w_mkv):
    B, S, _ = zm.shape
    M = mem.shape[1]
    q = zm.reshape(B, S, N_MEM_HEADS, HEAD_DIM)
    kv = rms_norm(mem, g) @ w_mkv
    km = kv[..., :MEM_W].reshape(B, M, N_MEM_HEADS, HEAD_DIM)
    vm = kv[..., MEM_W:].reshape(B, M, N_MEM_HEADS, HEAD_DIM)
    s = jnp.einsum('bshe,bmhe->bhsm', q, km).astype(jnp.float32) * (HEAD_DIM ** -0.5)
    p = jax.nn.softmax(s, axis=-1).astype(vm.dtype)
    return jnp.einsum('bhsm,bmhe->bshe', p, vm).reshape(B, S, MEM_W)


def setup_inputs(seed: int = 0) -> dict:
    key = jax.random.key(seed)
    ks = jax.random.split(key, 14)
    f32 = jnp.float32
    nrm = lambda k, shape, fan: jax.random.normal(k, shape, f32) * (fan ** -0.5)
    return {
        "x": jax.random.normal(ks[0], (BATCH, SEQ, D_MODEL), f32),
        "mem": jax.random.normal(ks[1], (BATCH, N_MEM, D_MODEL), f32),
        "positions": jnp.broadcast_to(jnp.arange(SEQ, dtype=jnp.int32)[None], (BATCH, SEQ)),
        "norm_gains": 1.0 + 0.05 * jax.random.normal(ks[2], (DEPTH, 4, D_MODEL), f32),
        "mem_norm": 1.0 + 0.05 * jax.random.normal(ks[3], (DEPTH, D_MODEL), f32),
        "w_in": nrm(ks[4], (DEPTH, D_MODEL, MIX_W), D_MODEL),
        "w_mem_kv": nrm(ks[5], (DEPTH, D_MODEL, 2 * MEM_W), D_MODEL),
        "w_out": nrm(ks[6], (DEPTH, MIX_W, D_MODEL), MIX_W),
        "w_pool": nrm(ks[7], (N_A_LAYERS, len(POOL_WINDOWS), POOL_GROUP, POOL_GROUP), POOL_GROUP),
        "pool_scale": 1.0 + 0.1 * jax.random.normal(ks[8], (N_A_LAYERS, MAIN_W), f32),
        "kv_norm": 1.0 + 0.05 * jax.random.normal(ks[9], (D_MODEL,), f32),
        "w_kv": nrm(ks[10], (D_MODEL, 2 * MAIN_W), D_MODEL),
        "w_gate_up": nrm(ks[11], (DEPTH, D_MODEL, 2 * D_FF), D_MODEL),
        "w_down": nrm(ks[12], (DEPTH, D_FF, D_MODEL), D_FF),
    }


def reference(x, mem, positions, norm_gains, mem_norm, w_in, w_mem_kv, w_out,
              w_pool, pool_scale, kv_norm, w_kv, w_gate_up, w_down):
    shared = None
    for l in range(DEPTH):
        h = rms_norm(x, norm_gains[l, 0])
        z = h @ w_in[l]
        z_main, z_mem = z[..., :MAIN_W], z[..., MAIN_W:]
        if l < N_A_LAYERS:
            y_main = pool_mixer(z_main, w_pool[l], pool_scale[l])
        else:
            y_main = dilated_mixer(z_main, positions, shared)
        y_mem = memory_attn(z_mem, mem, mem_norm[l], w_mem_kv[l])
        y = jnp.concatenate([y_main, y_mem], axis=-1) @ w_out[l]
        x = x + rms_norm(y, norm_gains[l, 1])
        h = rms_norm(x, norm_gains[l, 2])
        gu = h @ w_gate_up[l]
        y = (jax.nn.silu(gu[..., :D_FF]) * gu[..., D_FF:]) @ w_down[l]
        x = x + rms_norm(y, norm_gains[l, 3])
        if l == N_A_LAYERS - 1:
            shared = build_shared_kv(x, positions, kv_norm, w_kv)
    return x
```

```python
import functools

import jax
import jax.numpy as jnp
from jax import lax
from jax.experimental import pallas as pl
from jax.experimental.pallas import tpu as pltpu

D_MODEL = 1024
DEPTH = 4
N_MEM = 256
HEAD_DIM = 64
N_MEM_HEADS = 4
MEM_W = N_MEM_HEADS * HEAD_DIM
MAIN_W = D_MODEL - MEM_W
POOL_WINDOWS = (2, 4, 8, 16)
POOL_GROUP = MAIN_W // len(POOL_WINDOWS)
POOL_HALO = max(POOL_WINDOWS)
DIL_PATTERNS = ((128, 1), (512, 4), (2048, 16))
N_GROUPS = len(DIL_PATTERNS)
GROUP_W = MAIN_W // N_GROUPS
N_A_LAYERS = DEPTH // 2
D_FF = ((8 * D_MODEL + 3 * 256 - 1) // (3 * 256)) * 256
ROPE_THETA = 10000.0
EPS = 1e-6
NEG = -1e30
Q_SCALE = HEAD_DIM ** -0.5

LANES = 128
ATTN_BLOCK = 128
TOKEN_TILE = 512
VMEM_LIMIT = 56 * 1024 * 1024

BF16 = jnp.bfloat16
F32 = jnp.float32


def _rms(xf, gain):
    return xf * lax.rsqrt(jnp.mean(xf * xf, axis=-1, keepdims=True) + EPS) * gain


def _params(n_grid_dims):
    return pltpu.CompilerParams(
        dimension_semantics=("arbitrary",) * n_grid_dims,
        vmem_limit_bytes=VMEM_LIMIT)


def _head0_lanes():
    return lax.broadcasted_iota(jnp.int32, (1, LANES), 1) < HEAD_DIM


def _mem_kv_body(mem_ref, g_ref, w_ref, o_ref):
    h = _rms(mem_ref[...], g_ref[...]).astype(BF16)
    o_ref[...] = jnp.dot(h, w_ref[...], preferred_element_type=F32).astype(BF16)


def _mem_kv(mem, mem_norm, w_mem_kv):
    B = mem.shape[0]
    return pl.pallas_call(
        _mem_kv_body,
        grid=(DEPTH, B),
        in_specs=[
            pl.BlockSpec((None, N_MEM, D_MODEL), lambda l, b: (b, 0, 0)),
            pl.BlockSpec((None, 1, D_MODEL), lambda l, b: (l, 0, 0)),
            pl.BlockSpec((None, D_MODEL, 2 * MEM_W), lambda l, b: (l, 0, 0)),
        ],
        out_specs=pl.BlockSpec((None, None, N_MEM, 2 * MEM_W), lambda l, b: (l, b, 0, 0)),
        out_shape=jax.ShapeDtypeStruct((DEPTH, B, N_MEM, 2 * MEM_W), BF16),
        compiler_params=_params(2),
        name="mem_kv",
    )(mem, mem_norm.reshape(DEPTH, 1, D_MODEL), w_mem_kv)


def _memory_attention(zm, kvm):
    head0 = _head0_lanes()
    q = (zm * Q_SCALE).astype(BF16)
    pairs = []
    for p in range(MEM_W // LANES):
        qp = q[:, p * LANES:(p + 1) * LANES]
        kp = kvm[:, p * LANES:(p + 1) * LANES]
        vp = kvm[:, MEM_W + p * LANES:MEM_W + (p + 1) * LANES]
        outs = []
        for hh in range(2):
            lanes = head0 if hh == 0 else jnp.logical_not(head0)
            qm = jnp.where(lanes, qp, jnp.zeros_like(qp))
            s = lax.dot_general(qm, kp, (((1,), (1,)), ((), ())), preferred_element_type=F32)
            m = jnp.max(s, axis=-1, keepdims=True)
            e = jnp.exp(s - m)
            l = jnp.sum(e, axis=-1, keepdims=True)
            o = jnp.dot(e.astype(BF16), vp, preferred_element_type=F32)
            outs.append(o * (1.0 / l))
        pairs.append(jnp.where(head0, outs[0], outs[1]))
    return jnp.concatenate(pairs, axis=1)


def _rope_tables_body(pos_ref, freq_ref, sign_ref, cos_ref, sin_ref):
    ang = pos_ref[...].astype(F32) * freq_ref[...]
    cos_ref[...] = jnp.cos(ang)
    sin_ref[...] = jnp.sin(ang) * sign_ref[...]


def _rope_tables(positions):
    B, S = positions.shape
    half = HEAD_DIM // 2
    freqs = ROPE_THETA ** (-jnp.arange(half, dtype=F32) / half)
    lane = jnp.arange(LANES)
    freq_row = freqs[(lane % HEAD_DIM) % half].reshape(1, LANES)
    sign_row = jnp.where((lane % HEAD_DIM) < half, -1.0, 1.0).astype(F32).reshape(1, LANES)
    row = pl.BlockSpec((1, LANES), lambda b: (0, 0))
    tab = pl.BlockSpec((None, S, LANES), lambda b: (b, 0, 0))
    return pl.pallas_call(
        _rope_tables_body,
        grid=(B,),
        in_specs=[pl.BlockSpec((None, S, 1), lambda b: (b, 0, 0)), row, row],
        out_specs=[tab, tab],
        out_shape=[jax.ShapeDtypeStruct((B, S, LANES), F32)] * 2,
        compiler_params=_params(1),
        name="rope_tables",
    )(positions.reshape(B, S, 1), freq_row, sign_row)


def _rope(t, cos, sin_signed):
    first_half = (lax.broadcasted_iota(jnp.int32, (1, LANES), 1) % HEAD_DIM) < (HEAD_DIM // 2)
    cols = []
    for c in range(t.shape[1] // LANES):
        tc = t[:, c * LANES:(c + 1) * LANES]
        partner = jnp.where(first_half,
                            pltpu.roll(tc, LANES - HEAD_DIM // 2, 1),
                            pltpu.roll(tc, HEAD_DIM // 2, 1))
        cols.append(tc * cos + partner * sin_signed)
    return jnp.concatenate(cols, axis=1)


def _pool_means(zh):
    lane = lax.broadcasted_iota(jnp.int32, (1, LANES), 1)
    low = lane < (POOL_GROUP - LANES)
    s2 = zh + pltpu.roll(zh, 1, 0)
    a = s2[:, LANES:]
    s4 = a + pltpu.roll(a, 2, 0)
    a = s4[:, 2 * LANES:]
    s8 = a + pltpu.roll(a, 4, 0)
    a = s8[:, LANES:]
    s16 = a + pltpu.roll(a, 8, 0)
    h = POOL_HALO
    blk = lambda s, c: s[h:, c * LANES:(c + 1) * LANES]
    cols = [
        blk(s2, 0) * 0.5,
        jnp.where(low, blk(s2, 1) * 0.5, blk(s4, 0) * 0.25),
        blk(s4, 1) * 0.25,
        blk(s8, 0) * 0.125,
        jnp.where(low, blk(s8, 1) * 0.125, blk(s16, 0) * 0.0625),
        blk(s16, 1) * 0.0625,
    ]
    return jnp.concatenate(cols, axis=1)


def _mix_a_body(x_ref, g0_ref, g1_ref, w_in_ref, w_pool_ref, scale_ref, kvm_ref, w_out_ref,
                o_ref, zbuf):
    j = pl.program_id(1)
    T = x_ref.shape[0]
    x = x_ref[...]
    h = _rms(x, g0_ref[...]).astype(BF16)
    z = jnp.dot(h, w_in_ref[...], preferred_element_type=F32)
    u = z[:, :MAIN_W]

    @pl.when(j == 0)
    def _():
        zbuf[0:POOL_HALO, :] = jnp.zeros((POOL_HALO, MAIN_W), F32)

    zbuf[POOL_HALO:, :] = u
    mean = _pool_means(zbuf[...])
    t = lax.broadcasted_iota(jnp.int32, (POOL_HALO, MAIN_W), 0) + j * T
    col = lax.broadcasted_iota(jnp.int32, (POOL_HALO, MAIN_W), 1)
    win = jnp.full((POOL_HALO, MAIN_W), POOL_WINDOWS[0], jnp.int32)
    for gi in range(1, len(POOL_WINDOWS)):
        win = jnp.where(col >= gi * POOL_GROUP, POOL_WINDOWS[gi], win)
    fix = win.astype(F32) / jnp.minimum(t + 1, win).astype(F32)
    mean = jnp.concatenate([mean[:POOL_HALO] * fix, mean[POOL_HALO:]], axis=0)
    zbuf[0:POOL_HALO, :] = zbuf[T:T + POOL_HALO, :]

    p = (mean - u).astype(BF16)
    y_main = jnp.dot(p, w_pool_ref[...], preferred_element_type=F32) * scale_ref[...]
    y_mem = _memory_attention(z[:, MAIN_W:], kvm_ref[...])
    y = jnp.concatenate([y_main, y_mem], axis=1).astype(BF16)
    y = jnp.dot(y, w_out_ref[...], preferred_element_type=F32)
    o_ref[...] = x + _rms(y, g1_ref[...])


def _mix_a(x, g0, g1, w_in, w_pool_bd, pool_scale, kvm, w_out, l):
    B, S, _ = x.shape
    T = TOKEN_TILE
    vec = lambda n: pl.BlockSpec((1, n), lambda b, j: (0, 0))
    tok = pl.BlockSpec((None, T, D_MODEL), lambda b, j: (b, j, 0))
    return pl.pallas_call(
        _mix_a_body,
        grid=(B, S // T),
        in_specs=[
            tok, vec(D_MODEL), vec(D_MODEL),
            pl.BlockSpec((None, D_MODEL, D_MODEL), lambda b, j: (l, 0, 0)),
            pl.BlockSpec((None, MAIN_W, MAIN_W), lambda b, j: (l, 0, 0)),
            vec(MAIN_W),
            pl.BlockSpec((None, None, N_MEM, 2 * MEM_W), lambda b, j: (l, b, 0, 0)),
            pl.BlockSpec((None, D_MODEL, D_MODEL), lambda b, j: (l, 0, 0)),
        ],
        out_specs=tok,
        out_shape=jax.ShapeDtypeStruct(x.shape, F32),
        scratch_shapes=[pltpu.VMEM((T + POOL_HALO, MAIN_W), F32)],
        compiler_params=_params(2),
        name="mix_a",
    )(x, g0, g1, w_in, w_pool_bd, pool_scale, kvm, w_out)


def _ffn_body(x_ref, g2_ref, g3_ref, w_gu_ref, w_down_ref, o_ref):
    x = x_ref[...]
    h = _rms(x, g2_ref[...]).astype(BF16)
    gu = jnp.dot(h, w_gu_ref[...], preferred_element_type=F32)
    gate = gu[:, :D_FF]
    act = (gate * jax.nn.sigmoid(gate) * gu[:, D_FF:]).astype(BF16)
    y = jnp.dot(act, w_down_ref[...], preferred_element_type=F32)
    o_ref[...] = x + _rms(y, g3_ref[...])


def _ffn(x, g2, g3, w_gu, w_down, l):
    B, S, _ = x.shape
    T = TOKEN_TILE
    n = B * S
    vec = pl.BlockSpec((1, D_MODEL), lambda i: (0, 0))
    tok = pl.BlockSpec((T, D_MODEL), lambda i: (i, 0))
    once = pl.Buffered(1)
    out = pl.pallas_call(
        _ffn_body,
        grid=(n // T,),
        in_specs=[
            tok, vec, vec,
            pl.BlockSpec((None, D_MODEL, 2 * D_FF), lambda i: (l, 0, 0), pipeline_mode=once),
            pl.BlockSpec((None, D_FF, D_MODEL), lambda i: (l, 0, 0), pipeline_mode=once),
        ],
        out_specs=tok,
        out_shape=jax.ShapeDtypeStruct((n, D_MODEL), F32),
        compiler_params=_params(1),
        name="ffn",
    )(x.reshape(n, D_MODEL), g2, g3, w_gu, w_down)
    return out.reshape(B, S, D_MODEL)


def _shared_kv_body(x_ref, g_ref, w_ref, cos_ref, sin_ref, k_ref, v_ref):
    h = _rms(x_ref[...], g_ref[...]).astype(BF16)
    kv = jnp.dot(h, w_ref[...], preferred_element_type=F32)
    k = _rope(kv[:, :MAIN_W], cos_ref[...], sin_ref[...])
    for g in range(N_GROUPS):
        k_ref[g] = k[:, g * GROUP_W:(g + 1) * GROUP_W].astype(BF16)
        v_ref[g] = kv[:, MAIN_W + g * GROUP_W:MAIN_W + (g + 1) * GROUP_W].astype(BF16)


def _shared_kv(x, kv_norm, w_kv, cos_t, sin_t):
    B, S, _ = x.shape
    T = TOKEN_TILE
    grp = pl.BlockSpec((N_GROUPS, None, T, GROUP_W), lambda b, j: (0, b, j, 0))
    tab = pl.BlockSpec((None, T, LANES), lambda b, j: (b, j, 0))
    return pl.pallas_call(
        _shared_kv_body,
        grid=(B, S // T),
        in_specs=[
            pl.BlockSpec((None, T, D_MODEL), lambda b, j: (b, j, 0)),
            pl.BlockSpec((1, D_MODEL), lambda b, j: (0, 0)),
            pl.BlockSpec((D_MODEL, 2 * MAIN_W), lambda b, j: (0, 0)),
            tab, tab,
        ],
        out_specs=[grp, grp],
        out_shape=[jax.ShapeDtypeStruct((N_GROUPS, B, S, GROUP_W), BF16)] * 2,
        compiler_params=_params(2),
        name="shared_kv",
    )(x, kv_norm, w_kv, cos_t, sin_t)


def _mix_b_in_body(x_ref, g0_ref, w_in_ref, cos_ref, sin_ref, kvm_ref, q_ref, ymem_ref):
    h = _rms(x_ref[...], g0_ref[...]).astype(BF16)
    z = jnp.dot(h, w_in_ref[...], preferred_element_type=F32)
    q = _rope(z[:, :MAIN_W], cos_ref[...], sin_ref[...]) * Q_SCALE
    for g in range(N_GROUPS):
        q_ref[g] = q[:, g * GROUP_W:(g + 1) * GROUP_W].astype(BF16)
    ymem_ref[...] = _memory_attention(z[:, MAIN_W:], kvm_ref[...]).astype(BF16)


def _mix_b_in(x, g0, w_in, cos_t, sin_t, kvm, l):
    B, S, _ = x.shape
    T = TOKEN_TILE
    tab = pl.BlockSpec((None, T, LANES), lambda b, j: (b, j, 0))
    return pl.pallas_call(
        _mix_b_in_body,
        grid=(B, S // T),
        in_specs=[
            pl.BlockSpec((None, T, D_MODEL), lambda b, j: (b, j, 0)),
            pl.BlockSpec((1, D_MODEL), lambda b, j: (0, 0)),
            pl.BlockSpec((None, D_MODEL, D_MODEL), lambda b, j: (l, 0, 0)),
            tab, tab,
            pl.BlockSpec((None, None, N_MEM, 2 * MEM_W), lambda b, j: (l, b, 0, 0)),
        ],
        out_specs=[
            pl.BlockSpec((N_GROUPS, None, T, GROUP_W), lambda b, j: (0, b, j, 0)),
            pl.BlockSpec((None, T, MEM_W), lambda b, j: (b, j, 0)),
        ],
        out_shape=[
            jax.ShapeDtypeStruct((N_GROUPS, B, S, GROUP_W), BF16),
            jax.ShapeDtypeStruct((B, S, MEM_W), BF16),
        ],
        compiler_params=_params(2),
        name="mix_b_in",
    )(x, g0, w_in, cos_t, sin_t, kvm)


def _band_attn_body(q_ref, k_ref, v_ref, o_ref, lse_ref, *, n_seq, n_blk):
    A = ATTN_BLOCK
    head0 = _head0_lanes()
    qi = lax.broadcasted_iota(jnp.int32, (A, 2 * A), 0)
    kj = lax.broadcasted_iota(jnp.int32, (A, 2 * A), 1)
    band = (kj >= qi) & (kj <= qi + A)
    causal = (lax.broadcasted_iota(jnp.int32, (A, A), 1)
              <= lax.broadcasted_iota(jnp.int32, (A, A), 0))
    for r in range(n_seq):
        for n in range(n_blk):
            rows = slice(n * A, (n + 1) * A)
            krows = slice(0, A) if n == 0 else slice((n - 1) * A, (n + 1) * A)
            mask = causal if n == 0 else band
            for p in range(GROUP_W // LANES):
                cols = slice(r * GROUP_W + p * LANES, r * GROUP_W + (p + 1) * LANES)
                q = q_ref[rows, cols]
                k = k_ref[krows, cols]
                v = v_ref[krows, cols]
                outs, lses = [], []
                for hh in range(2):
                    lanes = head0 if hh == 0 else jnp.logical_not(head0)
                    qm = jnp.where(lanes, q, jnp.zeros_like(q))
                    s = lax.dot_general(qm, k, (((1,), (1,)), ((), ())), preferred_element_type=F32)
                    s = jnp.where(mask, s, NEG)
                    m = jnp.max(s, axis=-1, keepdims=True)
                    e = jnp.exp(s - m)
                    l = jnp.sum(e, axis=-1, keepdims=True)
                    o = jnp.dot(e.astype(BF16), v, preferred_element_type=F32)
                    outs.append(o * (1.0 / l))
                    lses.append(m + jnp.log(l))
                o_ref[rows, cols] = jnp.where(head0, outs[0], outs[1]).astype(BF16)
                lse_ref[rows, cols] = jnp.where(head0, lses[0], lses[1])


def _band_attn(qg, kg, vg, g):
    _, B, S, _ = qg.shape
    win, d = DIL_PATTERNS[g]
    assert win // d == ATTN_BLOCK and (S // d) % ATTN_BLOCK == 0
    L = S // d
    view = lambda a: a.reshape(N_GROUPS, B, L, d * GROUP_W)
    src = pl.BlockSpec((None, None, L, d * GROUP_W), lambda b: (g, b, 0, 0))
    dst = pl.BlockSpec((None, L, d * GROUP_W), lambda b: (b, 0, 0))
    o, lse = pl.pallas_call(
        functools.partial(_band_attn_body, n_seq=d, n_blk=L // ATTN_BLOCK),
        grid=(B,),
        in_specs=[src, src, src],
        out_specs=[dst, dst],
        out_shape=[
            jax.ShapeDtypeStruct((B, L, d * GROUP_W), BF16),
            jax.ShapeDtypeStruct((B, L, d * GROUP_W), F32),
        ],
        compiler_params=_params(1),
        name=f"band_attn_g{g}",
    )(view(qg), view(kg), view(vg))
    return o.reshape(B, S, GROUP_W), lse.reshape(B, S, GROUP_W)


def _mix_b_out_body(x_ref, g1_ref, o0_ref, o1_ref, o2_ref, l0_ref, l1_ref, l2_ref, ymem_ref,
                    w_out_ref, out_ref):
    lses = [l0_ref[...], l1_ref[...], l2_ref[...]]
    m = jnp.maximum(jnp.maximum(lses[0], lses[1]), lses[2])
    es = [jnp.exp(l - m) for l in lses]
    inv = 1.0 / (es[0] + es[1] + es[2])
    outs = [o0_ref[...], o1_ref[...], o2_ref[...]]
    cols = [(outs[g].astype(F32) * (es[g] * inv)).astype(BF16) for g in range(N_GROUPS)]
    y = jnp.concatenate(cols + [ymem_ref[...]], axis=1)
    y = jnp.dot(y, w_out_ref[...], preferred_element_type=F32)
    x = x_ref[...]
    out_ref[...] = x + _rms(y, g1_ref[...])


def _mix_b_out(x, g1, outs, lses, ymem, w_out, l):
    B, S, _ = x.shape
    T = TOKEN_TILE
    tok = pl.BlockSpec((None, T, D_MODEL), lambda b, j: (b, j, 0))
    grp = pl.BlockSpec((None, T, GROUP_W), lambda b, j: (b, j, 0))
    return pl.pallas_call(
        _mix_b_out_body,
        grid=(B, S // T),
        in_specs=[
            tok, pl.BlockSpec((1, D_MODEL), lambda b, j: (0, 0)),
            grp, grp, grp, grp, grp, grp,
            pl.BlockSpec((None, T, MEM_W), lambda b, j: (b, j, 0)),
            pl.BlockSpec((None, D_MODEL, D_MODEL), lambda b, j: (l, 0, 0)),
        ],
        out_specs=tok,
        out_shape=jax.ShapeDtypeStruct(x.shape, F32),
        compiler_params=_params(2),
        name="mix_b_out",
    )(x, g1, *outs, *lses, ymem, w_out)


def _pool_block_diag(w_pool):
    n_layers, n_grp = w_pool.shape[:2]
    out = jnp.zeros((n_layers, MAIN_W, MAIN_W), w_pool.dtype)
    for g in range(n_grp):
        sl = slice(g * POOL_GROUP, (g + 1) * POOL_GROUP)
        out = out.at[:, sl, sl].set(w_pool[:, g])
    return out


@jax.jit
def kernel(x, mem, positions, norm_gains, mem_norm, w_in, w_mem_kv, w_out, w_pool, pool_scale,
           kv_norm, w_kv, w_gate_up, w_down):
    w_in_b = w_in.astype(BF16)
    w_out_b = w_out.astype(BF16)
    w_gu_b = w_gate_up.astype(BF16)
    w_down_b = w_down.astype(BF16)
    w_pool_b = _pool_block_diag(w_pool).astype(BF16)
    gain = lambda l, i: norm_gains[l, i].reshape(1, D_MODEL)

    kvm = _mem_kv(mem, mem_norm, w_mem_kv.astype(BF16))
    cos_t, sin_t = _rope_tables(positions)
    kg = vg = None
    for l in range(DEPTH):
        if l < N_A_LAYERS:
            x = _mix_a(x, gain(l, 0), gain(l, 1), w_in_b, w_pool_b,
                       pool_scale[l].reshape(1, MAIN_W), kvm, w_out_b, l)
        else:
            qg, ymem = _mix_b_in(x, gain(l, 0), w_in_b, cos_t, sin_t, kvm, l)
            res = [_band_attn(qg, kg, vg, g) for g in range(N_GROUPS)]
            x = _mix_b_out(x, gain(l, 1), [r[0] for r in res], [r[1] for r in res], ymem, w_out_b, l)
        x = _ffn(x, gain(l, 2), gain(l, 3), w_gu_b, w_down_b, l)
        if l == N_A_LAYERS - 1:
            kg, vg = _shared_kv(x, kv_norm.reshape(1, D_MODEL), w_kv.astype(BF16), cos_t, sin_t)
    return x
```

```python
import jax
import jax.numpy as jnp
from jax import lax
from jax.experimental import pallas as pl
from jax.experimental.pallas import tpu as pltpu

D_MODEL = 1024
DEPTH = 4
N_MEM = 256
HEAD_DIM = 64
N_MEM_HEADS = 4
MEM_W = N_MEM_HEADS * HEAD_DIM
MAIN_W = D_MODEL - MEM_W
POOL_WINDOWS = (2, 4, 8, 16)
POOL_GROUP = MAIN_W // len(POOL_WINDOWS)
POOL_HALO = max(POOL_WINDOWS)
DIL_PATTERNS = ((128, 1), (512, 4), (2048, 16))
N_GROUPS = len(DIL_PATTERNS)
GROUP_W = MAIN_W // N_GROUPS
N_A_LAYERS = DEPTH // 2
D_FF = ((8 * D_MODEL + 3 * 256 - 1) // (3 * 256)) * 256
ROPE_THETA = 10000.0
EPS = 1e-6
NEG = -1e30
Q_SCALE = HEAD_DIM ** -0.5

LANES = 128
ATTN_BLOCK = 128
TOKEN_TILE = 512
VMEM_LIMIT = 56 * 1024 * 1024

BF16 = jnp.bfloat16
F32 = jnp.float32


def _rms(xf, gain):
    return xf * lax.rsqrt(jnp.mean(xf * xf, axis=-1, keepdims=True) + EPS) * gain


def _params(n_grid_dims):
    return pltpu.CompilerParams(
        dimension_semantics=("arbitrary",) * n_grid_dims,
        vmem_limit_bytes=VMEM_LIMIT)


def _head0_lanes():
    return lax.broadcasted_iota(jnp.int32, (1, LANES), 1) < HEAD_DIM


MEM_ROWS = 4 * N_MEM


def _mem_kv_body(mem_ref, g_ref, w_ref, o_ref):
    x = mem_ref[...]
    xn = x * lax.rsqrt(jnp.mean(x * x, axis=-1, keepdims=True) + EPS)
    for l in range(DEPTH):
        h = (xn * g_ref[l]).astype(BF16)
        o_ref[l] = jnp.dot(h, w_ref[l], preferred_element_type=F32).astype(BF16)


def _mem_kv(mem, mem_norm, w_mem_kv):
    rows = mem.shape[0] * N_MEM
    return pl.pallas_call(
        _mem_kv_body,
        grid=(rows // MEM_ROWS,),
        in_specs=[
            pl.BlockSpec((MEM_ROWS, D_MODEL), lambda i: (i, 0)),
            pl.BlockSpec((DEPTH, 1, D_MODEL), lambda i: (0, 0, 0)),
            pl.BlockSpec((DEPTH, D_MODEL, 2 * MEM_W), lambda i: (0, 0, 0)),
        ],
        out_specs=pl.BlockSpec((DEPTH, MEM_ROWS, 2 * MEM_W), lambda i: (0, i, 0)),
        out_shape=jax.ShapeDtypeStruct((DEPTH, rows, 2 * MEM_W), BF16),
        compiler_params=_params(1),
        name="mem_kv",
    )(mem.reshape(rows, D_MODEL), mem_norm.reshape(DEPTH, 1, D_MODEL), w_mem_kv)


def _kvm_spec(l):
    return pl.BlockSpec((None, N_MEM, 2 * MEM_W), lambda b, j: (l, b, 0))


def _split_heads(q):
    head0 = _head0_lanes()
    zero = jnp.zeros_like(q)
    return jnp.concatenate([jnp.where(head0, q, zero), jnp.where(head0, zero, q)], axis=0)


def _merge_heads(a):
    t = a.shape[0] // 2
    return jnp.where(_head0_lanes(), a[:t], a[t:])


def _with_ones(v):
    return jnp.concatenate([v, jnp.ones_like(v)], axis=1)


def _memory_attention(zm, kvm):
    q = (zm * Q_SCALE).astype(BF16)
    pairs = []
    for p in range(MEM_W // LANES):
        kp = kvm[:, p * LANES:(p + 1) * LANES]
        vp = kvm[:, MEM_W + p * LANES:MEM_W + (p + 1) * LANES]
        s = lax.dot_general(_split_heads(q[:, p * LANES:(p + 1) * LANES]), kp,
                            (((1,), (1,)), ((), ())), preferred_element_type=F32)
        e = jnp.exp(s - jnp.max(s, axis=-1, keepdims=True)).astype(BF16)
        oe = jnp.dot(e, _with_ones(vp), preferred_element_type=F32)
        pairs.append(_merge_heads(oe[:, :LANES]) / _merge_heads(oe[:, LANES:]))
    return jnp.concatenate(pairs, axis=1)


def _rope_tables_body(pos_ref, freq_ref, sign_ref, cos_ref, sin_ref):
    ang = pos_ref[...].astype(F32) * freq_ref[...]
    cos_ref[...] = jnp.cos(ang)
    sin_ref[...] = jnp.sin(ang) * sign_ref[...]


def _rope_tables(positions):
    B, S = positions.shape
    half = HEAD_DIM // 2
    freqs = ROPE_THETA ** (-jnp.arange(half, dtype=F32) / half)
    lane = jnp.arange(LANES)
    freq_row = freqs[(lane % HEAD_DIM) % half].reshape(1, LANES)
    sign_row = jnp.where((lane % HEAD_DIM) < half, -1.0, 1.0).astype(F32).reshape(1, LANES)
    row = pl.BlockSpec((1, LANES), lambda b: (0, 0))
    tab = pl.BlockSpec((None, S, LANES), lambda b: (b, 0, 0))
    return pl.pallas_call(
        _rope_tables_body,
        grid=(B,),
        in_specs=[pl.BlockSpec((None, S, 1), lambda b: (b, 0, 0)), row, row],
        out_specs=[tab, tab],
        out_shape=[jax.ShapeDtypeStruct((B, S, LANES), F32)] * 2,
        compiler_params=_params(1),
        name="rope_tables",
    )(positions.reshape(B, S, 1), freq_row, sign_row)


def _rope(t, cos, sin_signed):
    first_half = (lax.broadcasted_iota(jnp.int32, (1, LANES), 1) % HEAD_DIM) < (HEAD_DIM // 2)
    cols = []
    for c in range(t.shape[1] // LANES):
        tc = t[:, c * LANES:(c + 1) * LANES]
        partner = jnp.where(first_half,
                            pltpu.roll(tc, LANES - HEAD_DIM // 2, 1),
                            pltpu.roll(tc, HEAD_DIM // 2, 1))
        cols.append(tc * cos + partner * sin_signed)
    return jnp.concatenate(cols, axis=1)


def _pool_means(zh):
    lane = lax.broadcasted_iota(jnp.int32, (1, LANES), 1)
    low = lane < (POOL_GROUP - LANES)
    s2 = zh + pltpu.roll(zh, 1, 0)
    a = s2[:, LANES:]
    s4 = a + pltpu.roll(a, 2, 0)
    a = s4[:, 2 * LANES:]
    s8 = a + pltpu.roll(a, 4, 0)
    a = s8[:, LANES:]
    s16 = a + pltpu.roll(a, 8, 0)
    h = POOL_HALO
    blk = lambda s, c: s[h:, c * LANES:(c + 1) * LANES]
    cols = [
        blk(s2, 0) * 0.5,
        jnp.where(low, blk(s2, 1) * 0.5, blk(s4, 0) * 0.25),
        blk(s4, 1) * 0.25,
        blk(s8, 0) * 0.125,
        jnp.where(low, blk(s8, 1) * 0.125, blk(s16, 0) * 0.0625),
        blk(s16, 1) * 0.0625,
    ]
    return jnp.concatenate(cols, axis=1)


def _mix_a_body(x_ref, g0_ref, g1_ref, w_in_ref, w_pool_ref, scale_ref, kvm_ref, w_out_ref,
                o_ref, zbuf):
    j = pl.program_id(1)
    T = x_ref.shape[0]
    x = x_ref[...]
    h = _rms(x, g0_ref[...]).astype(BF16)
    z = jnp.dot(h, w_in_ref[...], preferred_element_type=F32)
    u = z[:, :MAIN_W]

    @pl.when(j == 0)
    def _():
        zbuf[0:POOL_HALO, :] = jnp.zeros((POOL_HALO, MAIN_W), F32)

    zbuf[POOL_HALO:, :] = u
    mean = _pool_means(zbuf[...])
    t = lax.broadcasted_iota(jnp.int32, (POOL_HALO, MAIN_W), 0) + j * T
    col = lax.broadcasted_iota(jnp.int32, (POOL_HALO, MAIN_W), 1)
    win = jnp.full((POOL_HALO, MAIN_W), POOL_WINDOWS[0], jnp.int32)
    for gi in range(1, len(POOL_WINDOWS)):
        win = jnp.where(col >= gi * POOL_GROUP, POOL_WINDOWS[gi], win)
    fix = win.astype(F32) / jnp.minimum(t + 1, win).astype(F32)
    mean = jnp.concatenate([mean[:POOL_HALO] * fix, mean[POOL_HALO:]], axis=0)
    zbuf[0:POOL_HALO, :] = zbuf[T:T + POOL_HALO, :]

    p = (mean - u).astype(BF16)
    y_main = jnp.dot(p, w_pool_ref[...], preferred_element_type=F32) * scale_ref[...]
    y_mem = _memory_attention(z[:, MAIN_W:], kvm_ref[...])
    y = jnp.concatenate([y_main, y_mem], axis=1).astype(BF16)
    y = jnp.dot(y, w_out_ref[...], preferred_element_type=F32)
    o_ref[...] = x + _rms(y, g1_ref[...])


def _mix_a(x, g0, g1, w_in, w_pool_bd, pool_scale, kvm, w_out, l):
    B, S, _ = x.shape
    T = TOKEN_TILE
    vec = lambda n: pl.BlockSpec((1, n), lambda b, j: (0, 0))
    tok = pl.BlockSpec((None, T, D_MODEL), lambda b, j: (b, j, 0))
    return pl.pallas_call(
        _mix_a_body,
        grid=(B, S // T),
        in_specs=[
            tok, vec(D_MODEL), vec(D_MODEL),
            pl.BlockSpec((None, D_MODEL, D_MODEL), lambda b, j: (l, 0, 0)),
            pl.BlockSpec((None, MAIN_W, MAIN_W), lambda b, j: (l, 0, 0)),
            vec(MAIN_W),
            _kvm_spec(l),
            pl.BlockSpec((None, D_MODEL, D_MODEL), lambda b, j: (l, 0, 0)),
        ],
        out_specs=tok,
        out_shape=jax.ShapeDtypeStruct(x.shape, F32),
        scratch_shapes=[pltpu.VMEM((T + POOL_HALO, MAIN_W), F32)],
        compiler_params=_params(2),
        name="mix_a",
    )(x, g0, g1, w_in, w_pool_bd, pool_scale, kvm, w_out)


def _ffn_body(x_ref, g2_ref, g3_ref, w_gu_ref, w_down_ref, o_ref):
    x = x_ref[...]
    h = _rms(x, g2_ref[...]).astype(BF16)
    gu = jnp.dot(h, w_gu_ref[...], preferred_element_type=F32)
    gate = gu[:, :D_FF]
    act = (gate * jax.nn.sigmoid(gate) * gu[:, D_FF:]).astype(BF16)
    y = jnp.dot(act, w_down_ref[...], preferred_element_type=F32)
    o_ref[...] = x + _rms(y, g3_ref[...])


def _ffn(x, g2, g3, w_gu, w_down, l):
    B, S, _ = x.shape
    T = TOKEN_TILE
    n = B * S
    vec = pl.BlockSpec((1, D_MODEL), lambda i: (0, 0))
    tok = pl.BlockSpec((T, D_MODEL), lambda i: (i, 0))
    once = pl.Buffered(1)
    out = pl.pallas_call(
        _ffn_body,
        grid=(n // T,),
        in_specs=[
            tok, vec, vec,
            pl.BlockSpec((None, D_MODEL, 2 * D_FF), lambda i: (l, 0, 0), pipeline_mode=once),
            pl.BlockSpec((None, D_FF, D_MODEL), lambda i: (l, 0, 0), pipeline_mode=once),
        ],
        out_specs=tok,
        out_shape=jax.ShapeDtypeStruct((n, D_MODEL), F32),
        compiler_params=_params(1),
        name="ffn",
    )(x.reshape(n, D_MODEL), g2, g3, w_gu, w_down)
    return out.reshape(B, S, D_MODEL)


def _dil_shape(B, S, g):
    d = DIL_PATTERNS[g][1]
    return (B, d, S // d, GROUP_W)


def _dil_spec(g, T):
    d = DIL_PATTERNS[g][1]
    return pl.BlockSpec((None, d, T // d, GROUP_W), lambda b, j: (b, 0, j, 0))


def _scatter_by_residue(val, slabs, out_ref, g):
    d = DIL_PATTERNS[g][1]
    T = val.shape[0]
    if d == 1:
        out_ref[0] = val.astype(BF16)
        return
    for p in range(GROUP_W // LANES):
        slabs[p] = val[:, p * LANES:(p + 1) * LANES]
    for r in range(d):
        for p in range(GROUP_W // LANES):
            out_ref[r, :, p * LANES:(p + 1) * LANES] = slabs[p, pl.ds(r, T // d, stride=d), :].astype(BF16)


def _gather_by_residue(in_ref, slabs, g):
    d, n, _ = in_ref.shape
    if d == 1:
        return in_ref[0].astype(F32)
    for r in range(d):
        for p in range(GROUP_W // LANES):
            slabs[p, pl.ds(r, n, stride=d), :] = in_ref[r, :, p * LANES:(p + 1) * LANES].astype(F32)
    return jnp.concatenate([slabs[p] for p in range(GROUP_W // LANES)], axis=1)


def _slab_scratch(n, T):
    return pltpu.VMEM((n, GROUP_W // LANES, T, LANES), F32)


def _shared_kv_body(x_ref, g_ref, w_ref, cos_ref, sin_ref, k0, k1, k2, v0, v1, v2, slabs):
    h = _rms(x_ref[...], g_ref[...]).astype(BF16)
    kv = jnp.dot(h, w_ref[...], preferred_element_type=F32)
    k = _rope(kv[:, :MAIN_W], cos_ref[...], sin_ref[...])
    for g, (k_ref, v_ref) in enumerate(((k0, v0), (k1, v1), (k2, v2))):
        _scatter_by_residue(k[:, g * GROUP_W:(g + 1) * GROUP_W], slabs.at[0], k_ref, g)
        _scatter_by_residue(kv[:, MAIN_W + g * GROUP_W:MAIN_W + (g + 1) * GROUP_W], slabs.at[1], v_ref, g)


def _shared_kv(x, kv_norm, w_kv, cos_t, sin_t):
    B, S, _ = x.shape
    T = TOKEN_TILE
    tab = pl.BlockSpec((None, T, LANES), lambda b, j: (b, j, 0))
    res = pl.pallas_call(
        _shared_kv_body,
        grid=(B, S // T),
        in_specs=[
            pl.BlockSpec((None, T, D_MODEL), lambda b, j: (b, j, 0)),
            pl.BlockSpec((1, D_MODEL), lambda b, j: (0, 0)),
            pl.BlockSpec((D_MODEL, 2 * MAIN_W), lambda b, j: (0, 0)),
            tab, tab,
        ],
        out_specs=[_dil_spec(g, T) for g in range(N_GROUPS)] * 2,
        out_shape=[jax.ShapeDtypeStruct(_dil_shape(B, S, g), BF16) for g in range(N_GROUPS)] * 2,
        scratch_shapes=[_slab_scratch(2, T)],
        compiler_params=_params(2),
        name="shared_kv",
    )(x, kv_norm, w_kv, cos_t, sin_t)
    return res[:N_GROUPS], res[N_GROUPS:]


def _mix_b_in_body(x_ref, g0_ref, w_in_ref, cos_ref, sin_ref, kvm_ref, q0, q1, q2, ymem_ref, slabs):
    h = _rms(x_ref[...], g0_ref[...]).astype(BF16)
    z = jnp.dot(h, w_in_ref[...], preferred_element_type=F32)
    q = _rope(z[:, :MAIN_W], cos_ref[...], sin_ref[...]) * Q_SCALE
    for g, q_ref in enumerate((q0, q1, q2)):
        _scatter_by_residue(q[:, g * GROUP_W:(g + 1) * GROUP_W], slabs.at[0], q_ref, g)
    ymem_ref[...] = _memory_attention(z[:, MAIN_W:], kvm_ref[...]).astype(BF16)


def _mix_b_in(x, g0, w_in, cos_t, sin_t, kvm, l):
    B, S, _ = x.shape
    T = TOKEN_TILE
    tab = pl.BlockSpec((None, T, LANES), lambda b, j: (b, j, 0))
    res = pl.pallas_call(
        _mix_b_in_body,
        grid=(B, S // T),
        in_specs=[
            pl.BlockSpec((None, T, D_MODEL), lambda b, j: (b, j, 0)),
            pl.BlockSpec((1, D_MODEL), lambda b, j: (0, 0)),
            pl.BlockSpec((None, D_MODEL, D_MODEL), lambda b, j: (l, 0, 0)),
            tab, tab,
            _kvm_spec(l),
        ],
        out_specs=[_dil_spec(g, T) for g in range(N_GROUPS)]
        + [pl.BlockSpec((None, T, MEM_W), lambda b, j: (b, j, 0))],
        out_shape=[jax.ShapeDtypeStruct(_dil_shape(B, S, g), BF16) for g in range(N_GROUPS)]
        + [jax.ShapeDtypeStruct((B, S, MEM_W), BF16)],
        scratch_shapes=[_slab_scratch(1, T)],
        compiler_params=_params(2),
        name="mix_b_in",
    )(x, g0, w_in, cos_t, sin_t, kvm)
    return res[:N_GROUPS], res[N_GROUPS]


def _band_attn_body(q_ref, k_ref, v_ref, o_ref, lse_ref):
    A = ATTN_BLOCK
    n_seq, L, _ = q_ref.shape
    qi = lax.broadcasted_iota(jnp.int32, (2 * A, 2 * A), 0) & (A - 1)
    kj = lax.broadcasted_iota(jnp.int32, (2 * A, 2 * A), 1)
    band = (kj >= qi) & (kj <= qi + A)
    causal = (lax.broadcasted_iota(jnp.int32, (2 * A, A), 1)
              <= (lax.broadcasted_iota(jnp.int32, (2 * A, A), 0) & (A - 1)))
    for r in range(n_seq):
        for n in range(L // A):
            rows = slice(n * A, (n + 1) * A)
            krows = slice(0, A) if n == 0 else slice((n - 1) * A, (n + 1) * A)
            mask = causal if n == 0 else band
            for p in range(GROUP_W // LANES):
                cols = slice(p * LANES, (p + 1) * LANES)
                s = lax.dot_general(_split_heads(q_ref[r, rows, cols]), k_ref[r, krows, cols],
                                    (((1,), (1,)), ((), ())), preferred_element_type=F32)
                s = jnp.where(mask, s, NEG)
                m = jnp.max(s, axis=-1, keepdims=True)
                e = jnp.exp(s - m).astype(BF16)
                oe = jnp.dot(e, _with_ones(v_ref[r, krows, cols]), preferred_element_type=F32)
                den = _merge_heads(oe[:, LANES:])
                o_ref[r, rows, cols] = (_merge_heads(oe[:, :LANES]) / den).astype(BF16)
                lse_ref[r, rows, cols] = _merge_heads(jnp.broadcast_to(m, (2 * A, LANES))) + jnp.log(den)


def _band_attn(q, k, v, g):
    B, d, L, _ = q.shape
    assert DIL_PATTERNS[g][0] // d == ATTN_BLOCK and L % ATTN_BLOCK == 0
    spec = pl.BlockSpec((None, d, L, GROUP_W), lambda b: (b, 0, 0, 0))
    return pl.pallas_call(
        _band_attn_body,
        grid=(B,),
        in_specs=[spec, spec, spec],
        out_specs=[spec, spec],
        out_shape=[jax.ShapeDtypeStruct(q.shape, BF16), jax.ShapeDtypeStruct(q.shape, F32)],
        compiler_params=_params(1),
        name=f"band_attn_g{g}",
    )(q, k, v)


def _mix_b_out_body(x_ref, g1_ref, o0_ref, o1_ref, o2_ref, l0_ref, l1_ref, l2_ref, ymem_ref,
                    w_out_ref, out_ref, slabs):
    outs = [_gather_by_residue(r, slabs.at[2 * g], g) for g, r in enumerate((o0_ref, o1_ref, o2_ref))]
    lses = [_gather_by_residue(r, slabs.at[2 * g + 1], g) for g, r in enumerate((l0_ref, l1_ref, l2_ref))]
    m = jnp.maximum(jnp.maximum(lses[0], lses[1]), lses[2])
    es = [jnp.exp(l - m) for l in lses]
    inv = 1.0 / (es[0] + es[1] + es[2])
    cols = [(outs[g] * (es[g] * inv)).astype(BF16) for g in range(N_GROUPS)]
    y = jnp.concatenate(cols + [ymem_ref[...]], axis=1)
    y = jnp.dot(y, w_out_ref[...], preferred_element_type=F32)
    x = x_ref[...]
    out_ref[...] = x + _rms(y, g1_ref[...])


def _mix_b_out(x, g1, outs, lses, ymem, w_out, l):
    B, S, _ = x.shape
    T = TOKEN_TILE
    tok = pl.BlockSpec((None, T, D_MODEL), lambda b, j: (b, j, 0))
    grp = [_dil_spec(g, T) for g in range(N_GROUPS)]
    return pl.pallas_call(
        _mix_b_out_body,
        grid=(B, S // T),
        in_specs=[
            tok, pl.BlockSpec((1, D_MODEL), lambda b, j: (0, 0)),
            *grp, *grp,
            pl.BlockSpec((None, T, MEM_W), lambda b, j: (b, j, 0)),
            pl.BlockSpec((None, D_MODEL, D_MODEL), lambda b, j: (l, 0, 0)),
        ],
        out_specs=tok,
        out_shape=jax.ShapeDtypeStruct(x.shape, F32),
        scratch_shapes=[_slab_scratch(2 * N_GROUPS, T)],
        compiler_params=_params(2),
        name="mix_b_out",
    )(x, g1, *outs, *lses, ymem, w_out)


def _pool_block_diag(w_pool):
    n_layers, n_grp = w_pool.shape[:2]
    out = jnp.zeros((n_layers, MAIN_W, MAIN_W), w_pool.dtype)
    for g in range(n_grp):
        sl = slice(g * POOL_GROUP, (g + 1) * POOL_GROUP)
        out = out.at[:, sl, sl].set(w_pool[:, g])
    return out


@jax.jit
def kernel(x, mem, positions, norm_gains, mem_norm, w_in, w_mem_kv, w_out, w_pool, pool_scale,
           kv_norm, w_kv, w_gate_up, w_down):
    w_in_b = w_in.astype(BF16)
    w_out_b = w_out.astype(BF16)
    w_gu_b = w_gate_up.astype(BF16)
    w_down_b = w_down.astype(BF16)
    w_pool_b = _pool_block_diag(w_pool).astype(BF16)
    gain = lambda l, i: norm_gains[l, i].reshape(1, D_MODEL)

    kvm = _mem_kv(mem, mem_norm, w_mem_kv.astype(BF16))
    cos_t, sin_t = _rope_tables(positions)
    kg = vg = None
    for l in range(DEPTH):
        if l < N_A_LAYERS:
            x = _mix_a(x, gain(l, 0), gain(l, 1), w_in_b, w_pool_b,
                       pool_scale[l].reshape(1, MAIN_W), kvm, w_out_b, l)
        else:
            qg, ymem = _mix_b_in(x, gain(l, 0), w_in_b, cos_t, sin_t, kvm, l)
            res = [_band_attn(qg[g], kg[g], vg[g], g) for g in range(N_GROUPS)]
            x = _mix_b_out(x, gain(l, 1), [r[0] for r in res], [r[1] for r in res], ymem, w_out_b, l)
        x = _ffn(x, gain(l, 2), gain(l, 3), w_gu_b, w_down_b, l)
        if l == N_A_LAYERS - 1:
            kg, vg = _shared_kv(x, kv_norm.reshape(1, D_MODEL), w_kv.astype(BF16), cos_t, sin_t)
    return x
```

```python
import jax
import jax.numpy as jnp
from jax import lax
from jax.experimental import pallas as pl
from jax.experimental.pallas import tpu as pltpu

D_MODEL = 1024
DEPTH = 4
N_MEM = 256
HEAD_DIM = 64
N_MEM_HEADS = 4
MEM_W = N_MEM_HEADS * HEAD_DIM
MAIN_W = D_MODEL - MEM_W
POOL_WINDOWS = (2, 4, 8, 16)
POOL_GROUP = MAIN_W // len(POOL_WINDOWS)
POOL_HALO = max(POOL_WINDOWS)
DIL_PATTERNS = ((128, 1), (512, 4), (2048, 16))
N_GROUPS = len(DIL_PATTERNS)
GROUP_W = MAIN_W // N_GROUPS
N_A_LAYERS = DEPTH // 2
D_FF = ((8 * D_MODEL + 3 * 256 - 1) // (3 * 256)) * 256
ROPE_THETA = 10000.0
EPS = 1e-6
NEG = -1e30
Q_SCALE = HEAD_DIM ** -0.5

LANES = 128
ATTN_BLOCK = 128
TOKEN_TILE = 512
MIX_TILE = 1024
SUB_TILE = 512
VMEM_LIMIT = 56 * 1024 * 1024

BF16 = jnp.bfloat16
F32 = jnp.float32


def _rms(xf, gain):
    return xf * lax.rsqrt(jnp.mean(xf * xf, axis=-1, keepdims=True) + EPS) * gain


def _params(n_grid_dims):
    return pltpu.CompilerParams(
        dimension_semantics=("arbitrary",) * n_grid_dims,
        vmem_limit_bytes=VMEM_LIMIT)


def _head0_lanes():
    return lax.broadcasted_iota(jnp.int32, (1, LANES), 1) < HEAD_DIM


MEM_ROWS = 4 * N_MEM


def _mem_kv_body(mem_ref, g_ref, w_ref, o_ref):
    x = mem_ref[...]
    xn = x * lax.rsqrt(jnp.mean(x * x, axis=-1, keepdims=True) + EPS)
    for l in range(DEPTH):
        h = (xn * g_ref[l]).astype(BF16)
        o_ref[l] = jnp.dot(h, w_ref[l], preferred_element_type=F32).astype(BF16)


def _mem_kv(mem, mem_norm, w_mem_kv):
    rows = mem.shape[0] * N_MEM
    return pl.pallas_call(
        _mem_kv_body,
        grid=(rows // MEM_ROWS,),
        in_specs=[
            pl.BlockSpec((MEM_ROWS, D_MODEL), lambda i: (i, 0)),
            pl.BlockSpec((DEPTH, 1, D_MODEL), lambda i: (0, 0, 0)),
            pl.BlockSpec((DEPTH, D_MODEL, 2 * MEM_W), lambda i: (0, 0, 0)),
        ],
        out_specs=pl.BlockSpec((DEPTH, MEM_ROWS, 2 * MEM_W), lambda i: (0, i, 0)),
        out_shape=jax.ShapeDtypeStruct((DEPTH, rows, 2 * MEM_W), BF16),
        compiler_params=_params(1),
        name="mem_kv",
    )(mem.reshape(rows, D_MODEL), mem_norm.reshape(DEPTH, 1, D_MODEL), w_mem_kv)


def _kvm_spec(l):
    return pl.BlockSpec((None, N_MEM, 2 * MEM_W), lambda b, j: (l, b, 0))


def _split_heads(q):
    head0 = _head0_lanes()
    zero = jnp.zeros_like(q)
    return jnp.concatenate([jnp.where(head0, q, zero), jnp.where(head0, zero, q)], axis=0)


def _merge_heads(a):
    t = a.shape[0] // 2
    return jnp.where(_head0_lanes(), a[:t], a[t:])


def _with_ones(v):
    return jnp.concatenate([v, jnp.ones_like(v)], axis=1)


def _memory_attention(zm, kvm):
    q = (zm * Q_SCALE).astype(BF16)
    pairs = []
    for p in range(MEM_W // LANES):
        kp = kvm[:, p * LANES:(p + 1) * LANES]
        vp = kvm[:, MEM_W + p * LANES:MEM_W + (p + 1) * LANES]
        s = lax.dot_general(_split_heads(q[:, p * LANES:(p + 1) * LANES]), kp,
                            (((1,), (1,)), ((), ())), preferred_element_type=F32)
        e = jnp.exp(s - jnp.max(s, axis=-1, keepdims=True)).astype(BF16)
        oe = jnp.dot(e, _with_ones(vp), preferred_element_type=F32)
        pairs.append(_merge_heads(oe[:, :LANES]) / _merge_heads(oe[:, LANES:]))
    return jnp.concatenate(pairs, axis=1)


def _rope_tables_body(pos_ref, freq_ref, sign_ref, cos_ref, sin_ref):
    ang = pos_ref[...].astype(F32) * freq_ref[...]
    cos_ref[...] = jnp.cos(ang)
    sin_ref[...] = jnp.sin(ang) * sign_ref[...]


def _rope_tables(positions):
    B, S = positions.shape
    half = HEAD_DIM // 2
    freqs = ROPE_THETA ** (-jnp.arange(half, dtype=F32) / half)
    lane = jnp.arange(LANES)
    freq_row = freqs[(lane % HEAD_DIM) % half].reshape(1, LANES)
    sign_row = jnp.where((lane % HEAD_DIM) < half, -1.0, 1.0).astype(F32).reshape(1, LANES)
    row = pl.BlockSpec((1, LANES), lambda b: (0, 0))
    tab = pl.BlockSpec((None, S, LANES), lambda b: (b, 0, 0))
    return pl.pallas_call(
        _rope_tables_body,
        grid=(B,),
        in_specs=[pl.BlockSpec((None, S, 1), lambda b: (b, 0, 0)), row, row],
        out_specs=[tab, tab],
        out_shape=[jax.ShapeDtypeStruct((B, S, LANES), F32)] * 2,
        compiler_params=_params(1),
        name="rope_tables",
    )(positions.reshape(B, S, 1), freq_row, sign_row)


def _rope(t, cos, sin_signed):
    first_half = (lax.broadcasted_iota(jnp.int32, (1, LANES), 1) % HEAD_DIM) < (HEAD_DIM // 2)
    cols = []
    for c in range(t.shape[1] // LANES):
        tc = t[:, c * LANES:(c + 1) * LANES]
        partner = jnp.where(first_half,
                            pltpu.roll(tc, LANES - HEAD_DIM // 2, 1),
                            pltpu.roll(tc, HEAD_DIM // 2, 1))
        cols.append(tc * cos + partner * sin_signed)
    return jnp.concatenate(cols, axis=1)


def _pool_means(zh):
    lane = lax.broadcasted_iota(jnp.int32, (1, LANES), 1)
    low = lane < (POOL_GROUP - LANES)
    s2 = zh + pltpu.roll(zh, 1, 0)
    a = s2[:, LANES:]
    s4 = a + pltpu.roll(a, 2, 0)
    a = s4[:, 2 * LANES:]
    s8 = a + pltpu.roll(a, 4, 0)
    a = s8[:, LANES:]
    s16 = a + pltpu.roll(a, 8, 0)
    h = POOL_HALO
    blk = lambda s, c: s[h:, c * LANES:(c + 1) * LANES]
    cols = [
        blk(s2, 0) * 0.5,
        jnp.where(low, blk(s2, 1) * 0.5, blk(s4, 0) * 0.25),
        blk(s4, 1) * 0.25,
        blk(s8, 0) * 0.125,
        jnp.where(low, blk(s8, 1) * 0.125, blk(s16, 0) * 0.0625),
        blk(s16, 1) * 0.0625,
    ]
    return jnp.concatenate(cols, axis=1)


def _mix_a_rows(i, n_rows, x_ref, g0_ref, g1_ref, w_in_ref, w_pool_ref, scale_ref, kvm_ref, w_out_ref,
                o_ref, zbuf):
    T = x_ref.shape[0]
    lo = i * n_rows
    x = x_ref[lo:lo + n_rows, :]
    h = _rms(x, g0_ref[...]).astype(BF16)
    z = jnp.dot(h, w_in_ref[...], preferred_element_type=F32)
    u = z[:, :MAIN_W]
    zbuf[POOL_HALO + lo:POOL_HALO + lo + n_rows, :] = u
    mean = _pool_means(zbuf[lo:lo + n_rows + POOL_HALO, :])
    t = lax.broadcasted_iota(jnp.int32, (POOL_HALO, MAIN_W), 0) + (pl.program_id(1) * T + lo)
    col = lax.broadcasted_iota(jnp.int32, (POOL_HALO, MAIN_W), 1)
    win = jnp.full((POOL_HALO, MAIN_W), POOL_WINDOWS[0], jnp.int32)
    for gi in range(1, len(POOL_WINDOWS)):
        win = jnp.where(col >= gi * POOL_GROUP, POOL_WINDOWS[gi], win)
    fix = win.astype(F32) / jnp.minimum(t + 1, win).astype(F32)
    mean = jnp.concatenate([mean[:POOL_HALO] * fix, mean[POOL_HALO:]], axis=0)

    p = (mean - u).astype(BF16)
    y_main = jnp.dot(p, w_pool_ref[...], preferred_element_type=F32) * scale_ref[...]
    y_mem = _memory_attention(z[:, MAIN_W:], kvm_ref[...])
    y = jnp.concatenate([y_main, y_mem], axis=1).astype(BF16)
    y = jnp.dot(y, w_out_ref[...], preferred_element_type=F32)
    o_ref[lo:lo + n_rows, :] = x + _rms(y, g1_ref[...])


def _mix_a_body(x_ref, g0_ref, g1_ref, w_in_ref, w_pool_ref, scale_ref, kvm_ref, w_out_ref,
                o_ref, zbuf):
    T = x_ref.shape[0]

    @pl.when(pl.program_id(1) == 0)
    def _():
        zbuf[0:POOL_HALO, :] = jnp.zeros((POOL_HALO, MAIN_W), F32)

    for i in range(T // SUB_TILE):
        _mix_a_rows(i, SUB_TILE, x_ref, g0_ref, g1_ref, w_in_ref, w_pool_ref, scale_ref, kvm_ref, w_out_ref,
                    o_ref, zbuf)
    zbuf[0:POOL_HALO, :] = zbuf[T:T + POOL_HALO, :]


def _mix_a(x, g0, g1, w_in, w_pool_bd, pool_scale, kvm, w_out, l):
    B, S, _ = x.shape
    T = MIX_TILE
    vec = lambda n: pl.BlockSpec((1, n), lambda b, j: (0, 0))
    tok = pl.BlockSpec((None, T, D_MODEL), lambda b, j: (b, j, 0))
    return pl.pallas_call(
        _mix_a_body,
        grid=(B, S // T),
        in_specs=[
            tok, vec(D_MODEL), vec(D_MODEL),
            pl.BlockSpec((None, D_MODEL, D_MODEL), lambda b, j: (l, 0, 0)),
            pl.BlockSpec((None, MAIN_W, MAIN_W), lambda b, j: (l, 0, 0)),
            vec(MAIN_W),
            _kvm_spec(l),
            pl.BlockSpec((None, D_MODEL, D_MODEL), lambda b, j: (l, 0, 0)),
        ],
        out_specs=tok,
        out_shape=jax.ShapeDtypeStruct(x.shape, F32),
        scratch_shapes=[pltpu.VMEM((T + POOL_HALO, MAIN_W), F32)],
        compiler_params=_params(2),
        name="mix_a",
    )(x, g0, g1, w_in, w_pool_bd, pool_scale, kvm, w_out)


def _ffn_body(x_ref, g2_ref, g3_ref, w_gu_ref, w_down_ref, o_ref):
    x = x_ref[...]
    h = _rms(x, g2_ref[...]).astype(BF16)
    gu = jnp.dot(h, w_gu_ref[...], preferred_element_type=F32)
    gate = gu[:, :D_FF]
    act = (gate * jax.nn.sigmoid(gate) * gu[:, D_FF:]).astype(BF16)
    y = jnp.dot(act, w_down_ref[...], preferred_element_type=F32)
    o_ref[...] = x + _rms(y, g3_ref[...])


def _ffn(x, g2, g3, w_gu, w_down, l):
    B, S, _ = x.shape
    T = TOKEN_TILE
    n = B * S
    vec = pl.BlockSpec((1, D_MODEL), lambda i: (0, 0))
    tok = pl.BlockSpec((T, D_MODEL), lambda i: (i, 0))
    once = pl.Buffered(1)
    out = pl.pallas_call(
        _ffn_body,
        grid=(n // T,),
        in_specs=[
            tok, vec, vec,
            pl.BlockSpec((None, D_MODEL, 2 * D_FF), lambda i: (l, 0, 0), pipeline_mode=once),
            pl.BlockSpec((None, D_FF, D_MODEL), lambda i: (l, 0, 0), pipeline_mode=once),
        ],
        out_specs=tok,
        out_shape=jax.ShapeDtypeStruct((n, D_MODEL), F32),
        compiler_params=_params(1),
        name="ffn",
    )(x.reshape(n, D_MODEL), g2, g3, w_gu, w_down)
    return out.reshape(B, S, D_MODEL)


def _dil_shape(B, S, g):
    d = DIL_PATTERNS[g][1]
    return (B, d, S // d, GROUP_W)


def _dil_spec(g, T):
    d = DIL_PATTERNS[g][1]
    return pl.BlockSpec((None, d, T // d, GROUP_W), lambda b, j: (b, 0, j, 0))


def _scatter_by_residue(val, slabs, out_ref, g, lo):
    d = DIL_PATTERNS[g][1]
    H = val.shape[0]
    if d == 1:
        out_ref[0, lo:lo + H, :] = val.astype(BF16)
        return
    for p in range(GROUP_W // LANES):
        slabs[p] = val[:, p * LANES:(p + 1) * LANES]
    for r in range(d):
        for p in range(GROUP_W // LANES):
            out_ref[r, lo // d:(lo + H) // d, p * LANES:(p + 1) * LANES] = (
                slabs[p, pl.ds(r, H // d, stride=d), :].astype(BF16))


def _gather_by_residue(in_ref, slabs, g, lo, H):
    d = DIL_PATTERNS[g][1]
    if d == 1:
        return in_ref[0, lo:lo + H, :].astype(F32)
    for r in range(d):
        for p in range(GROUP_W // LANES):
            slabs[p, pl.ds(r, H // d, stride=d), :] = (
                in_ref[r, lo // d:(lo + H) // d, p * LANES:(p + 1) * LANES].astype(F32))
    return jnp.concatenate([slabs[p] for p in range(GROUP_W // LANES)], axis=1)


def _slab_scratch(n_chains, n):
    return pltpu.VMEM((n_chains, n, GROUP_W // LANES, SUB_TILE, LANES), F32)


def _shared_kv_body(x_ref, g_ref, w_ref, cos_ref, sin_ref, k0, k1, k2, v0, v1, v2, slabs):
    for i in range(x_ref.shape[0] // SUB_TILE):
        lo = i * SUB_TILE
        rows = slice(lo, lo + SUB_TILE)
        h = _rms(x_ref[rows, :], g_ref[...]).astype(BF16)
        kv = jnp.dot(h, w_ref[...], preferred_element_type=F32)
        k = _rope(kv[:, :MAIN_W], cos_ref[rows, :], sin_ref[rows, :])
        for g, (k_ref, v_ref) in enumerate(((k0, v0), (k1, v1), (k2, v2))):
            _scatter_by_residue(k[:, g * GROUP_W:(g + 1) * GROUP_W], slabs.at[i, 0], k_ref, g, lo)
            _scatter_by_residue(kv[:, MAIN_W + g * GROUP_W:MAIN_W + (g + 1) * GROUP_W], slabs.at[i, 1],
                                v_ref, g, lo)


def _shared_kv(x, kv_norm, w_kv, cos_t, sin_t):
    B, S, _ = x.shape
    T = MIX_TILE
    tab = pl.BlockSpec((None, T, LANES), lambda b, j: (b, j, 0))
    res = pl.pallas_call(
        _shared_kv_body,
        grid=(B, S // T),
        in_specs=[
            pl.BlockSpec((None, T, D_MODEL), lambda b, j: (b, j, 0)),
            pl.BlockSpec((1, D_MODEL), lambda b, j: (0, 0)),
            pl.BlockSpec((D_MODEL, 2 * MAIN_W), lambda b, j: (0, 0)),
            tab, tab,
        ],
        out_specs=[_dil_spec(g, T) for g in range(N_GROUPS)] * 2,
        out_shape=[jax.ShapeDtypeStruct(_dil_shape(B, S, g), BF16) for g in range(N_GROUPS)] * 2,
        scratch_shapes=[_slab_scratch(T // SUB_TILE, 2)],
        compiler_params=_params(2),
        name="shared_kv",
    )(x, kv_norm, w_kv, cos_t, sin_t)
    return res[:N_GROUPS], res[N_GROUPS:]


def _mix_b_in_body(x_ref, g0_ref, w_in_ref, cos_ref, sin_ref, kvm_ref, q0, q1, q2, ymem_ref, slabs):
    for i in range(x_ref.shape[0] // SUB_TILE):
        lo = i * SUB_TILE
        rows = slice(lo, lo + SUB_TILE)
        h = _rms(x_ref[rows, :], g0_ref[...]).astype(BF16)
        z = jnp.dot(h, w_in_ref[...], preferred_element_type=F32)
        q = _rope(z[:, :MAIN_W], cos_ref[rows, :], sin_ref[rows, :]) * Q_SCALE
        for g, q_ref in enumerate((q0, q1, q2)):
            _scatter_by_residue(q[:, g * GROUP_W:(g + 1) * GROUP_W], slabs.at[i, 0], q_ref, g, lo)
        ymem_ref[rows, :] = _memory_attention(z[:, MAIN_W:], kvm_ref[...]).astype(BF16)


def _mix_b_in(x, g0, w_in, cos_t, sin_t, kvm, l):
    B, S, _ = x.shape
    T = MIX_TILE
    tab = pl.BlockSpec((None, T, LANES), lambda b, j: (b, j, 0))
    res = pl.pallas_call(
        _mix_b_in_body,
        grid=(B, S // T),
        in_specs=[
            pl.BlockSpec((None, T, D_MODEL), lambda b, j: (b, j, 0)),
            pl.BlockSpec((1, D_MODEL), lambda b, j: (0, 0)),
            pl.BlockSpec((None, D_MODEL, D_MODEL), lambda b, j: (l, 0, 0)),
            tab, tab,
            _kvm_spec(l),
        ],
        out_specs=[_dil_spec(g, T) for g in range(N_GROUPS)]
        + [pl.BlockSpec((None, T, MEM_W), lambda b, j: (b, j, 0))],
        out_shape=[jax.ShapeDtypeStruct(_dil_shape(B, S, g), BF16) for g in range(N_GROUPS)]
        + [jax.ShapeDtypeStruct((B, S, MEM_W), BF16)],
        scratch_shapes=[_slab_scratch(T // SUB_TILE, 1)],
        compiler_params=_params(2),
        name="mix_b_in",
    )(x, g0, w_in, cos_t, sin_t, kvm)
    return res[:N_GROUPS], res[N_GROUPS]


def _band_attn_body(q_ref, k_ref, v_ref, o_ref, lse_ref):
    A = ATTN_BLOCK
    n_seq, L, _ = q_ref.shape
    qi = lax.broadcasted_iota(jnp.int32, (2 * A, 2 * A), 0) & (A - 1)
    kj = lax.broadcasted_iota(jnp.int32, (2 * A, 2 * A), 1)
    band = (kj >= qi) & (kj <= qi + A)
    causal = (lax.broadcasted_iota(jnp.int32, (2 * A, A), 1)
              <= (lax.broadcasted_iota(jnp.int32, (2 * A, A), 0) & (A - 1)))
    for r in range(n_seq):
        for n in range(L // A):
            rows = slice(n * A, (n + 1) * A)
            krows = slice(0, A) if n == 0 else slice((n - 1) * A, (n + 1) * A)
            mask = causal if n == 0 else band
            for p in range(GROUP_W // LANES):
                cols = slice(p * LANES, (p + 1) * LANES)
                s = lax.dot_general(_split_heads(q_ref[r, rows, cols]), k_ref[r, krows, cols],
                                    (((1,), (1,)), ((), ())), preferred_element_type=F32)
                s = jnp.where(mask, s, NEG)
                m = jnp.max(s, axis=-1, keepdims=True)
                e = jnp.exp(s - m).astype(BF16)
                oe = jnp.dot(e, _with_ones(v_ref[r, krows, cols]), preferred_element_type=F32)
                den = _merge_heads(oe[:, LANES:])
                o_ref[r, rows, cols] = (_merge_heads(oe[:, :LANES]) / den).astype(BF16)
                lse_ref[r, rows, cols] = _merge_heads(jnp.broadcast_to(m, (2 * A, LANES))) + jnp.log(den)


def _band_attn(q, k, v, g):
    B, d, L, _ = q.shape
    assert DIL_PATTERNS[g][0] // d == ATTN_BLOCK and L % ATTN_BLOCK == 0
    spec = pl.BlockSpec((None, d, L, GROUP_W), lambda b: (b, 0, 0, 0))
    return pl.pallas_call(
        _band_attn_body,
        grid=(B,),
        in_specs=[spec, spec, spec],
        out_specs=[spec, spec],
        out_shape=[jax.ShapeDtypeStruct(q.shape, BF16), jax.ShapeDtypeStruct(q.shape, F32)],
        compiler_params=_params(1),
        name=f"band_attn_g{g}",
    )(q, k, v)


def _mix_b_out_body(x_ref, g1_ref, o0_ref, o1_ref, o2_ref, l0_ref, l1_ref, l2_ref, ymem_ref,
                    w_out_ref, out_ref, slabs):
    H = SUB_TILE
    for i in range(x_ref.shape[0] // H):
        lo = i * H
        rows = slice(lo, lo + H)
        outs = [_gather_by_residue(r, slabs.at[i, 2 * g], g, lo, H)
                for g, r in enumerate((o0_ref, o1_ref, o2_ref))]
        lses = [_gather_by_residue(r, slabs.at[i, 2 * g + 1], g, lo, H)
                for g, r in enumerate((l0_ref, l1_ref, l2_ref))]
        m = jnp.maximum(jnp.maximum(lses[0], lses[1]), lses[2])
        es = [jnp.exp(l - m) for l in lses]
        inv = 1.0 / (es[0] + es[1] + es[2])
        cols = [(outs[g] * (es[g] * inv)).astype(BF16) for g in range(N_GROUPS)]
        y = jnp.concatenate(cols + [ymem_ref[rows, :]], axis=1)
        y = jnp.dot(y, w_out_ref[...], preferred_element_type=F32)
        out_ref[rows, :] = x_ref[rows, :] + _rms(y, g1_ref[...])


def _mix_b_out(x, g1, outs, lses, ymem, w_out, l):
    B, S, _ = x.shape
    T = MIX_TILE
    tok = pl.BlockSpec((None, T, D_MODEL), lambda b, j: (b, j, 0))
    grp = [_dil_spec(g, T) for g in range(N_GROUPS)]
    return pl.pallas_call(
        _mix_b_out_body,
        grid=(B, S // T),
        in_specs=[
            tok, pl.BlockSpec((1, D_MODEL), lambda b, j: (0, 0)),
            *grp, *grp,
            pl.BlockSpec((None, T, MEM_W), lambda b, j: (b, j, 0)),
            pl.BlockSpec((None, D_MODEL, D_MODEL), lambda b, j: (l, 0, 0)),
        ],
        out_specs=tok,
        out_shape=jax.ShapeDtypeStruct(x.shape, F32),
        scratch_shapes=[_slab_scratch(T // SUB_TILE, 2 * N_GROUPS)],
        compiler_params=_params(2),
        name="mix_b_out",
    )(x, g1, *outs, *lses, ymem, w_out)


def _pool_block_diag(w_pool):
    n_layers, n_grp = w_pool.shape[:2]
    out = jnp.zeros((n_layers, MAIN_W, MAIN_W), w_pool.dtype)
    for g in range(n_grp):
        sl = slice(g * POOL_GROUP, (g + 1) * POOL_GROUP)
        out = out.at[:, sl, sl].set(w_pool[:, g])
    return out


@jax.jit
def kernel(x, mem, positions, norm_gains, mem_norm, w_in, w_mem_kv, w_out, w_pool, pool_scale,
           kv_norm, w_kv, w_gate_up, w_down):
    w_in_b = w_in.astype(BF16)
    w_out_b = w_out.astype(BF16)
    w_gu_b = w_gate_up.astype(BF16)
    w_down_b = w_down.astype(BF16)
    w_pool_b = _pool_block_diag(w_pool).astype(BF16)
    gain = lambda l, i: norm_gains[l, i].reshape(1, D_MODEL)

    kvm = _mem_kv(mem, mem_norm, w_mem_kv.astype(BF16))
    cos_t, sin_t = _rope_tables(positions)
    kg = vg = None
    for l in range(DEPTH):
        if l < N_A_LAYERS:
            x = _mix_a(x, gain(l, 0), gain(l, 1), w_in_b, w_pool_b,
                       pool_scale[l].reshape(1, MAIN_W), kvm, w_out_b, l)
        else:
            qg, ymem = _mix_b_in(x, gain(l, 0), w_in_b, cos_t, sin_t, kvm, l)
            res = [_band_attn(qg[g], kg[g], vg[g], g) for g in range(N_GROUPS)]
            x = _mix_b_out(x, gain(l, 1), [r[0] for r in res], [r[1] for r in res], ymem, w_out_b, l)
        x = _ffn(x, gain(l, 2), gain(l, 3), w_gu_b, w_down_b, l)
        if l == N_A_LAYERS - 1:
            kg, vg = _shared_kv(x, kv_norm.reshape(1, D_MODEL), w_kv.astype(BF16), cos_t, sin_t)
    return x
```

```python
import jax
import jax.numpy as jnp
from jax import lax
from jax.experimental import pallas as pl
from jax.experimental.pallas import tpu as pltpu

D_MODEL = 1024
DEPTH = 4
N_MEM = 256
HEAD_DIM = 64
N_MEM_HEADS = 4
MEM_W = N_MEM_HEADS * HEAD_DIM
MAIN_W = D_MODEL - MEM_W
POOL_WINDOWS = (2, 4, 8, 16)
POOL_GROUP = MAIN_W // len(POOL_WINDOWS)
POOL_HALO = max(POOL_WINDOWS)
POOL_PAIR_W = 2 * POOL_GROUP
DIL_PATTERNS = ((128, 1), (512, 4), (2048, 16))
N_GROUPS = len(DIL_PATTERNS)
GROUP_W = MAIN_W // N_GROUPS
N_A_LAYERS = DEPTH // 2
D_FF = ((8 * D_MODEL + 3 * 256 - 1) // (3 * 256)) * 256
ROPE_THETA = 10000.0
EPS = 1e-6
NEG = -1e30
Q_SCALE = HEAD_DIM ** -0.5

LANES = 128
ATTN_BLOCK = 128
FFN_TILE = 1024
FFN_CHAIN = 512
FF_CHUNKS = ((0, 1536), (1536, D_FF))
MIX_TILE = 1024
SUB_TILE = 512
VMEM_LIMIT = 56 * 1024 * 1024

BF16 = jnp.bfloat16
F32 = jnp.float32


def _rms(xf, gain):
    return xf * lax.rsqrt(jnp.mean(xf * xf, axis=-1, keepdims=True) + EPS) * gain


def _params(n_grid_dims):
    return pltpu.CompilerParams(
        dimension_semantics=("arbitrary",) * n_grid_dims,
        vmem_limit_bytes=VMEM_LIMIT)


def _head0_lanes():
    return lax.broadcasted_iota(jnp.int32, (1, LANES), 1) < HEAD_DIM


MEM_ROWS = 4 * N_MEM


def _mem_kv_body(mem_ref, g_ref, w_ref, o_ref):
    x = mem_ref[...]
    xn = x * lax.rsqrt(jnp.mean(x * x, axis=-1, keepdims=True) + EPS)
    for l in range(DEPTH):
        h = (xn * g_ref[l]).astype(BF16)
        o_ref[l] = jnp.dot(h, w_ref[l], preferred_element_type=F32).astype(BF16)


def _mem_kv(mem, mem_norm, w_mem_kv):
    rows = mem.shape[0] * N_MEM
    return pl.pallas_call(
        _mem_kv_body,
        grid=(rows // MEM_ROWS,),
        in_specs=[
            pl.BlockSpec((MEM_ROWS, D_MODEL), lambda i: (i, 0)),
            pl.BlockSpec((DEPTH, 1, D_MODEL), lambda i: (0, 0, 0)),
            pl.BlockSpec((DEPTH, D_MODEL, 2 * MEM_W), lambda i: (0, 0, 0)),
        ],
        out_specs=pl.BlockSpec((DEPTH, MEM_ROWS, 2 * MEM_W), lambda i: (0, i, 0)),
        out_shape=jax.ShapeDtypeStruct((DEPTH, rows, 2 * MEM_W), BF16),
        compiler_params=_params(1),
        name="mem_kv",
    )(mem.reshape(rows, D_MODEL), mem_norm.reshape(DEPTH, 1, D_MODEL), w_mem_kv)


def _kvm_spec(l):
    return pl.BlockSpec((None, N_MEM, 2 * MEM_W), lambda b, j: (l, b, 0))


def _split_heads(q):
    head0 = _head0_lanes()
    zero = jnp.zeros_like(q)
    return jnp.concatenate([jnp.where(head0, q, zero), jnp.where(head0, zero, q)], axis=0)


def _merge_heads(a):
    t = a.shape[0] // 2
    return jnp.where(_head0_lanes(), a[:t], a[t:])


def _with_ones(v):
    return jnp.concatenate([v, jnp.ones_like(v)], axis=1)


def _memory_attention(zm, kvm):
    q = (zm * Q_SCALE).astype(BF16)
    pairs = []
    for p in range(MEM_W // LANES):
        kp = kvm[:, p * LANES:(p + 1) * LANES]
        vp = kvm[:, MEM_W + p * LANES:MEM_W + (p + 1) * LANES]
        s = lax.dot_general(_split_heads(q[:, p * LANES:(p + 1) * LANES]), kp,
                            (((1,), (1,)), ((), ())), preferred_element_type=F32)
        e = jnp.exp(s - jnp.max(s, axis=-1, keepdims=True)).astype(BF16)
        oe = jnp.dot(e, _with_ones(vp), preferred_element_type=F32)
        pairs.append(_merge_heads(oe[:, :LANES]) / _merge_heads(oe[:, LANES:]))
    return jnp.concatenate(pairs, axis=1)


def _rope_tables_body(pos_ref, freq_ref, cos_ref, sin_ref):
    ang = freq_ref[...] * pos_ref[...].astype(F32)
    c = jnp.cos(ang)
    s = jnp.sin(ang)
    cos_ref[...] = jnp.concatenate([c, c, c, c], axis=0).T
    sin_ref[...] = jnp.concatenate([-s, s, -s, s], axis=0).T


def _rope_tables(positions):
    B, S = positions.shape
    half = HEAD_DIM // 2
    freqs = ROPE_THETA ** (-jnp.arange(half, dtype=F32) / half)
    tab = pl.BlockSpec((None, S, LANES), lambda b: (b, 0, 0))
    return pl.pallas_call(
        _rope_tables_body,
        grid=(B,),
        in_specs=[pl.BlockSpec((None, 1, S), lambda b: (b, 0, 0)),
                  pl.BlockSpec((half, 1), lambda b: (0, 0))],
        out_specs=[tab, tab],
        out_shape=[jax.ShapeDtypeStruct((B, S, LANES), F32)] * 2,
        compiler_params=_params(1),
        name="rope_tables",
    )(positions.reshape(B, 1, S), freqs.reshape(half, 1))


def _rope(t, cos, sin_signed):
    first_half = (lax.broadcasted_iota(jnp.int32, (1, LANES), 1) % HEAD_DIM) < (HEAD_DIM // 2)
    cols = []
    for c in range(t.shape[1] // LANES):
        tc = t[:, c * LANES:(c + 1) * LANES]
        partner = jnp.where(first_half,
                            pltpu.roll(tc, LANES - HEAD_DIM // 2, 1),
                            pltpu.roll(tc, HEAD_DIM // 2, 1))
        cols.append(tc * cos + partner * sin_signed)
    return jnp.concatenate(cols, axis=1)


def _pool_means(zh):
    lane = lax.broadcasted_iota(jnp.int32, (1, LANES), 1)
    low = lane < (POOL_GROUP - LANES)
    s2 = zh + pltpu.roll(zh, 1, 0)
    a = s2[:, LANES:]
    s4 = a + pltpu.roll(a, 2, 0)
    a = s4[:, 2 * LANES:]
    s8 = a + pltpu.roll(a, 4, 0)
    a = s8[:, LANES:]
    s16 = a + pltpu.roll(a, 8, 0)
    h = POOL_HALO
    blk = lambda s, c: s[h:, c * LANES:(c + 1) * LANES]
    cols = [
        blk(s2, 0) * 0.5,
        jnp.where(low, blk(s2, 1) * 0.5, blk(s4, 0) * 0.25),
        blk(s4, 1) * 0.25,
        blk(s8, 0) * 0.125,
        jnp.where(low, blk(s8, 1) * 0.125, blk(s16, 0) * 0.0625),
        blk(s16, 1) * 0.0625,
    ]
    return jnp.concatenate(cols, axis=1)


def _mix_a_rows(i, n_rows, x_ref, g0_ref, g1_ref, w_in_ref, w_pool_ref, scale_ref, kvm_ref, w_out_ref,
                o_ref, zbuf):
    T = x_ref.shape[0]
    lo = i * n_rows
    x = x_ref[lo:lo + n_rows, :]
    h = _rms(x, g0_ref[...]).astype(BF16)
    z = jnp.dot(h, w_in_ref[...], preferred_element_type=F32)
    u = z[:, :MAIN_W]
    zbuf[POOL_HALO + lo:POOL_HALO + lo + n_rows, :] = u
    mean = _pool_means(zbuf[lo:lo + n_rows + POOL_HALO, :])
    t = lax.broadcasted_iota(jnp.int32, (POOL_HALO, MAIN_W), 0) + (pl.program_id(1) * T + lo)
    col = lax.broadcasted_iota(jnp.int32, (POOL_HALO, MAIN_W), 1)
    win = jnp.full((POOL_HALO, MAIN_W), POOL_WINDOWS[0], jnp.int32)
    for gi in range(1, len(POOL_WINDOWS)):
        win = jnp.where(col >= gi * POOL_GROUP, POOL_WINDOWS[gi], win)
    fix = win.astype(F32) / jnp.minimum(t + 1, win).astype(F32)
    mean = jnp.concatenate([mean[:POOL_HALO] * fix, mean[POOL_HALO:]], axis=0)

    p = (mean - u).astype(BF16)
    y_main = jnp.concatenate(
        [jnp.dot(p[:, k * POOL_PAIR_W:(k + 1) * POOL_PAIR_W], w_pool_ref[k], preferred_element_type=F32)
         for k in range(MAIN_W // POOL_PAIR_W)], axis=1) * scale_ref[...]
    y_mem = _memory_attention(z[:, MAIN_W:], kvm_ref[...])
    y = jnp.concatenate([y_main, y_mem], axis=1).astype(BF16)
    y = jnp.dot(y, w_out_ref[...], preferred_element_type=F32)
    o_ref[lo:lo + n_rows, :] = x + _rms(y, g1_ref[...])


def _mix_a_body(x_ref, g0_ref, g1_ref, w_in_ref, w_pool_ref, scale_ref, kvm_ref, w_out_ref,
                o_ref, zbuf):
    T = x_ref.shape[0]

    @pl.when(pl.program_id(1) == 0)
    def _():
        zbuf[0:POOL_HALO, :] = jnp.zeros((POOL_HALO, MAIN_W), F32)

    for i in range(T // SUB_TILE):
        _mix_a_rows(i, SUB_TILE, x_ref, g0_ref, g1_ref, w_in_ref, w_pool_ref, scale_ref, kvm_ref, w_out_ref,
                    o_ref, zbuf)
    zbuf[0:POOL_HALO, :] = zbuf[T:T + POOL_HALO, :]


def _mix_a(x, g0, g1, w_in, w_pool_bd, pool_scale, kvm, w_out, l):
    B, S, _ = x.shape
    T = MIX_TILE
    vec = lambda n: pl.BlockSpec((1, n), lambda b, j: (0, 0))
    tok = pl.BlockSpec((None, T, D_MODEL), lambda b, j: (b, j, 0))
    return pl.pallas_call(
        _mix_a_body,
        grid=(B, S // T),
        in_specs=[
            tok, vec(D_MODEL), vec(D_MODEL),
            pl.BlockSpec((None, D_MODEL, D_MODEL), lambda b, j: (l, 0, 0)),
            pl.BlockSpec((None, MAIN_W // POOL_PAIR_W, POOL_PAIR_W, POOL_PAIR_W), lambda b, j: (l, 0, 0, 0)),
            vec(MAIN_W),
            _kvm_spec(l),
            pl.BlockSpec((None, D_MODEL, D_MODEL), lambda b, j: (l, 0, 0)),
        ],
        out_specs=tok,
        out_shape=jax.ShapeDtypeStruct(x.shape, F32),
        scratch_shapes=[pltpu.VMEM((T + POOL_HALO, MAIN_W), F32)],
        compiler_params=_params(2),
        name="mix_a",
    )(x, g0, g1, w_in, w_pool_bd, pool_scale, kvm, w_out)


def _ffn_body(x_ref, g2_ref, g3_ref, w_gu_ref, w_down_ref, o_ref):
    for i in range(x_ref.shape[0] // FFN_CHAIN):
        rows = slice(i * FFN_CHAIN, (i + 1) * FFN_CHAIN)
        x = x_ref[rows, :]
        h = _rms(x, g2_ref[...]).astype(BF16)
        y = None
        for lo, hi in FF_CHUNKS:
            gate = jnp.dot(h, w_gu_ref[:, lo:hi], preferred_element_type=F32)
            up = jnp.dot(h, w_gu_ref[:, D_FF + lo:D_FF + hi], preferred_element_type=F32)
            act = (gate * jax.nn.sigmoid(gate) * up).astype(BF16)
            part = jnp.dot(act, w_down_ref[lo:hi, :], preferred_element_type=F32)
            y = part if y is None else y + part
        o_ref[rows, :] = x + _rms(y, g3_ref[...])


def _ffn(x, g2, g3, w_gu, w_down, l):
    B, S, _ = x.shape
    T = FFN_TILE
    n = B * S
    vec = pl.BlockSpec((1, D_MODEL), lambda i: (0, 0))
    tok = pl.BlockSpec((T, D_MODEL), lambda i: (i, 0))
    once = pl.Buffered(1)
    out = pl.pallas_call(
        _ffn_body,
        grid=(n // T,),
        in_specs=[
            tok, vec, vec,
            pl.BlockSpec((None, D_MODEL, 2 * D_FF), lambda i: (l, 0, 0), pipeline_mode=once),
            pl.BlockSpec((None, D_FF, D_MODEL), lambda i: (l, 0, 0), pipeline_mode=once),
        ],
        out_specs=tok,
        out_shape=jax.ShapeDtypeStruct((n, D_MODEL), F32),
        compiler_params=_params(1),
        name="ffn",
    )(x.reshape(n, D_MODEL), g2, g3, w_gu, w_down)
    return out.reshape(B, S, D_MODEL)


def _dil_shape(B, S, g):
    d = DIL_PATTERNS[g][1]
    return (B, d, S // d, GROUP_W)


def _dil_spec(g, T):
    d = DIL_PATTERNS[g][1]
    return pl.BlockSpec((None, d, T // d, GROUP_W), lambda b, j: (b, 0, j, 0))


def _scatter_by_residue(val, slabs, out_ref, g, lo):
    d = DIL_PATTERNS[g][1]
    H = val.shape[0]
    if d == 1:
        out_ref[0, lo:lo + H, :] = val.astype(BF16)
        return
    for p in range(GROUP_W // LANES):
        slabs[p] = val[:, p * LANES:(p + 1) * LANES]
    for r in range(d):
        for p in range(GROUP_W // LANES):
            out_ref[r, lo // d:(lo + H) // d, p * LANES:(p + 1) * LANES] = (
                slabs[p, pl.ds(r, H // d, stride=d), :].astype(BF16))


def _gather_by_residue(in_ref, slabs, g, lo, H):
    d = DIL_PATTERNS[g][1]
    if d == 1:
        return in_ref[0, lo:lo + H, :].astype(F32)
    for r in range(d):
        for p in range(GROUP_W // LANES):
            slabs[p, pl.ds(r, H // d, stride=d), :] = (
                in_ref[r, lo // d:(lo + H) // d, p * LANES:(p + 1) * LANES].astype(F32))
    return jnp.concatenate([slabs[p] for p in range(GROUP_W // LANES)], axis=1)


def _slab_scratch(n_chains, n):
    return pltpu.VMEM((n_chains, n, GROUP_W // LANES, SUB_TILE, LANES), F32)


def _shared_kv_body(x_ref, g_ref, w_ref, cos_ref, sin_ref, k0, k1, k2, v0, v1, v2, slabs):
    for i in range(x_ref.shape[0] // SUB_TILE):
        lo = i * SUB_TILE
        rows = slice(lo, lo + SUB_TILE)
        h = _rms(x_ref[rows, :], g_ref[...]).astype(BF16)
        kv = jnp.dot(h, w_ref[...], preferred_element_type=F32)
        k = _rope(kv[:, :MAIN_W], cos_ref[rows, :], sin_ref[rows, :])
        for g, (k_ref, v_ref) in enumerate(((k0, v0), (k1, v1), (k2, v2))):
            _scatter_by_residue(k[:, g * GROUP_W:(g + 1) * GROUP_W], slabs.at[i, 0], k_ref, g, lo)
            _scatter_by_residue(kv[:, MAIN_W + g * GROUP_W:MAIN_W + (g + 1) * GROUP_W], slabs.at[i, 1],
                                v_ref, g, lo)


def _shared_kv(x, kv_norm, w_kv, cos_t, sin_t):
    B, S, _ = x.shape
    T = MIX_TILE
    tab = pl.BlockSpec((None, T, LANES), lambda b, j: (b, j, 0))
    res = pl.pallas_call(
        _shared_kv_body,
        grid=(B, S // T),
        in_specs=[
            pl.BlockSpec((None, T, D_MODEL), lambda b, j: (b, j, 0)),
            pl.BlockSpec((1, D_MODEL), lambda b, j: (0, 0)),
            pl.BlockSpec((D_MODEL, 2 * MAIN_W), lambda b, j: (0, 0)),
            tab, tab,
        ],
        out_specs=[_dil_spec(g, T) for g in range(N_GROUPS)] * 2,
        out_shape=[jax.ShapeDtypeStruct(_dil_shape(B, S, g), BF16) for g in range(N_GROUPS)] * 2,
        scratch_shapes=[_slab_scratch(T // SUB_TILE, 2)],
        compiler_params=_params(2),
        name="shared_kv",
    )(x, kv_norm, w_kv, cos_t, sin_t)
    return res[:N_GROUPS], res[N_GROUPS:]


def _mix_b_in_body(x_ref, g0_ref, w_in_ref, cos_ref, sin_ref, kvm_ref, q0, q1, q2, ymem_ref, slabs):
    for i in range(x_ref.shape[0] // SUB_TILE):
        lo = i * SUB_TILE
        rows = slice(lo, lo + SUB_TILE)
        h = _rms(x_ref[rows, :], g0_ref[...]).astype(BF16)
        z = jnp.dot(h, w_in_ref[...], preferred_element_type=F32)
        q = _rope(z[:, :MAIN_W], cos_ref[rows, :], sin_ref[rows, :]) * Q_SCALE
        for g, q_ref in enumerate((q0, q1, q2)):
            _scatter_by_residue(q[:, g * GROUP_W:(g + 1) * GROUP_W], slabs.at[i, 0], q_ref, g, lo)
        ymem_ref[rows, :] = _memory_attention(z[:, MAIN_W:], kvm_ref[...]).astype(BF16)


def _mix_b_in(x, g0, w_in, cos_t, sin_t, kvm, l):
    B, S, _ = x.shape
    T = MIX_TILE
    tab = pl.BlockSpec((None, T, LANES), lambda b, j: (b, j, 0))
    res = pl.pallas_call(
        _mix_b_in_body,
        grid=(B, S // T),
        in_specs=[
            pl.BlockSpec((None, T, D_MODEL), lambda b, j: (b, j, 0)),
            pl.BlockSpec((1, D_MODEL), lambda b, j: (0, 0)),
            pl.BlockSpec((None, D_MODEL, D_MODEL), lambda b, j: (l, 0, 0)),
            tab, tab,
            _kvm_spec(l),
        ],
        out_specs=[_dil_spec(g, T) for g in range(N_GROUPS)]
        + [pl.BlockSpec((None, T, MEM_W), lambda b, j: (b, j, 0))],
        out_shape=[jax.ShapeDtypeStruct(_dil_shape(B, S, g), BF16) for g in range(N_GROUPS)]
        + [jax.ShapeDtypeStruct((B, S, MEM_W), BF16)],
        scratch_shapes=[_slab_scratch(T // SUB_TILE, 1)],
        compiler_params=_params(2),
        name="mix_b_in",
    )(x, g0, w_in, cos_t, sin_t, kvm)
    return res[:N_GROUPS], res[N_GROUPS]


def _band_attn_body(q_ref, k_ref, v_ref, o_ref, lse_ref):
    A = ATTN_BLOCK
    n_seq, L, _ = q_ref.shape
    qi = lax.broadcasted_iota(jnp.int32, (2 * A, 2 * A), 0) & (A - 1)
    kj = lax.broadcasted_iota(jnp.int32, (2 * A, 2 * A), 1)
    band = (kj >= qi) & (kj <= qi + A)
    causal = (lax.broadcasted_iota(jnp.int32, (2 * A, A), 1)
              <= (lax.broadcasted_iota(jnp.int32, (2 * A, A), 0) & (A - 1)))
    qi2 = lax.broadcasted_iota(jnp.int32, (4 * A, 2 * A), 0)
    kj2 = lax.broadcasted_iota(jnp.int32, (4 * A, 2 * A), 1)
    causal2 = ((qi2 >= 2 * A) == (kj2 >= A)) & ((kj2 & (A - 1)) <= (qi2 & (A - 1)))

    first = [(r, 0) for r in range(n_seq)]
    units = [(first[i:i + 2], causal2) for i in range(0, n_seq - 1, 2)]
    if n_seq % 2:
        units.append((first[-1:], causal))
    units += [([(r, n)], band) for r in range(n_seq) for n in range(1, L // A)]

    for members, mask in units:
        krows = lambda n: slice(0, A) if n == 0 else slice((n - 1) * A, (n + 1) * A)
        for p in range(GROUP_W // LANES):
            cols = slice(p * LANES, (p + 1) * LANES)
            cat = lambda parts: parts[0] if len(parts) == 1 else jnp.concatenate(parts, axis=0)
            q = cat([_split_heads(q_ref[r, n * A:(n + 1) * A, cols]) for r, n in members])
            k = cat([k_ref[r, krows(n), cols] for r, n in members])
            v = cat([v_ref[r, krows(n), cols] for r, n in members])
            s = lax.dot_general(q, k, (((1,), (1,)), ((), ())), preferred_element_type=F32)
            s = jnp.where(mask, s, NEG)
            m = jnp.max(s, axis=-1, keepdims=True)
            e = jnp.exp(s - m).astype(BF16)
            oe = jnp.dot(e, _with_ones(v), preferred_element_type=F32)
            for i, (r, n) in enumerate(members):
                part = slice(i * 2 * A, (i + 1) * 2 * A)
                den = _merge_heads(oe[part, LANES:])
                o_ref[r, n * A:(n + 1) * A, cols] = (_merge_heads(oe[part, :LANES]) / den).astype(BF16)
                lse_ref[r, n * A:(n + 1) * A, cols] = (
                    _merge_heads(jnp.broadcast_to(m[part], (2 * A, LANES))) + jnp.log(den))


def _band_attn(q, k, v, g):
    B, d, L, _ = q.shape
    assert DIL_PATTERNS[g][0] // d == ATTN_BLOCK and L % ATTN_BLOCK == 0
    spec = pl.BlockSpec((None, d, L, GROUP_W), lambda b: (b, 0, 0, 0))
    return pl.pallas_call(
        _band_attn_body,
        grid=(B,),
        in_specs=[spec, spec, spec],
        out_specs=[spec, spec],
        out_shape=[jax.ShapeDtypeStruct(q.shape, BF16), jax.ShapeDtypeStruct(q.shape, F32)],
        compiler_params=_params(1),
        name=f"band_attn_g{g}",
    )(q, k, v)


def _mix_b_out_body(x_ref, g1_ref, o0_ref, o1_ref, o2_ref, l0_ref, l1_ref, l2_ref, ymem_ref,
                    w_out_ref, out_ref, slabs):
    H = SUB_TILE
    for i in range(x_ref.shape[0] // H):
        lo = i * H
        rows = slice(lo, lo + H)
        outs = [_gather_by_residue(r, slabs.at[i, 2 * g], g, lo, H)
                for g, r in enumerate((o0_ref, o1_ref, o2_ref))]
        lses = [_gather_by_residue(r, slabs.at[i, 2 * g + 1], g, lo, H)
                for g, r in enumerate((l0_ref, l1_ref, l2_ref))]
        m = jnp.maximum(jnp.maximum(lses[0], lses[1]), lses[2])
        es = [jnp.exp(l - m) for l in lses]
        inv = 1.0 / (es[0] + es[1] + es[2])
        cols = [(outs[g] * (es[g] * inv)).astype(BF16) for g in range(N_GROUPS)]
        y = jnp.concatenate(cols + [ymem_ref[rows, :]], axis=1)
        y = jnp.dot(y, w_out_ref[...], preferred_element_type=F32)
        out_ref[rows, :] = x_ref[rows, :] + _rms(y, g1_ref[...])


def _mix_b_out(x, g1, outs, lses, ymem, w_out, l):
    B, S, _ = x.shape
    T = MIX_TILE
    tok = pl.BlockSpec((None, T, D_MODEL), lambda b, j: (b, j, 0))
    grp = [_dil_spec(g, T) for g in range(N_GROUPS)]
    return pl.pallas_call(
        _mix_b_out_body,
        grid=(B, S // T),
        in_specs=[
            tok, pl.BlockSpec((1, D_MODEL), lambda b, j: (0, 0)),
            *grp, *grp,
            pl.BlockSpec((None, T, MEM_W), lambda b, j: (b, j, 0)),
            pl.BlockSpec((None, D_MODEL, D_MODEL), lambda b, j: (l, 0, 0)),
        ],
        out_specs=tok,
        out_shape=jax.ShapeDtypeStruct(x.shape, F32),
        scratch_shapes=[_slab_scratch(T // SUB_TILE, 2 * N_GROUPS)],
        compiler_params=_params(2),
        name="mix_b_out",
    )(x, g1, *outs, *lses, ymem, w_out)


def _pool_block_diag(w_pool):
    n_layers, n_grp = w_pool.shape[:2]
    out = jnp.zeros((n_layers, n_grp // 2, POOL_PAIR_W, POOL_PAIR_W), w_pool.dtype)
    for g in range(n_grp):
        sl = slice((g % 2) * POOL_GROUP, (g % 2 + 1) * POOL_GROUP)
        out = out.at[:, g // 2, sl, sl].set(w_pool[:, g])
    return out


@jax.jit
def kernel(x, mem, positions, norm_gains, mem_norm, w_in, w_mem_kv, w_out, w_pool, pool_scale,
           kv_norm, w_kv, w_gate_up, w_down):
    w_in_b = w_in.astype(BF16)
    w_out_b = w_out.astype(BF16)
    w_gu_b = w_gate_up.astype(BF16)
    w_down_b = w_down.astype(BF16)
    w_pool_b = _pool_block_diag(w_pool).astype(BF16)
    gain = lambda l, i: norm_gains[l, i].reshape(1, D_MODEL)

    kvm = _mem_kv(mem, mem_norm, w_mem_kv.astype(BF16))
    cos_t, sin_t = _rope_tables(positions)
    kg = vg = None
    for l in range(DEPTH):
        if l < N_A_LAYERS:
            x = _mix_a(x, gain(l, 0), gain(l, 1), w_in_b, w_pool_b,
                       pool_scale[l].reshape(1, MAIN_W), kvm, w_out_b, l)
        else:
            qg, ymem = _mix_b_in(x, gain(l, 0), w_in_b, cos_t, sin_t, kvm, l)
            res = [_band_attn(qg[g], kg[g], vg[g], g) for g in range(N_GROUPS)]
            x = _mix_b_out(x, gain(l, 1), [r[0] for r in res], [r[1] for r in res], ymem, w_out_b, l)
        x = _ffn(x, gain(l, 2), gain(l, 3), w_gu_b, w_down_b, l)
        if l == N_A_LAYERS - 1:
            kg, vg = _shared_kv(x, kv_norm.reshape(1, D_MODEL), w_kv.astype(BF16), cos_t, sin_t)
    return x
```

```python
import jax
import jax.numpy as jnp
from jax import lax
from jax.experimental import pallas as pl
from jax.experimental.pallas import tpu as pltpu

D_MODEL = 1024
DEPTH = 4
N_MEM = 256
HEAD_DIM = 64
N_MEM_HEADS = 4
MEM_W = N_MEM_HEADS * HEAD_DIM
MAIN_W = D_MODEL - MEM_W
POOL_WINDOWS = (2, 4, 8, 16)
POOL_GROUP = MAIN_W // len(POOL_WINDOWS)
POOL_HALO = max(POOL_WINDOWS)
DIL_PATTERNS = ((128, 1), (512, 4), (2048, 16))
N_GROUPS = len(DIL_PATTERNS)
GROUP_W = MAIN_W // N_GROUPS
N_A_LAYERS = DEPTH // 2
D_FF = ((8 * D_MODEL + 3 * 256 - 1) // (3 * 256)) * 256
ROPE_THETA = 10000.0
EPS = 1e-6
NEG = -1e30
Q_SCALE = HEAD_DIM ** -0.5

LANES = 128
ATTN_BLOCK = 128
FF_CHUNKS = ((0, 1536), (1536, D_FF))
A_TILE, A_CHAIN = 1024, 512
B_TILE, B_CHAIN = 1024, 512
MIX_TILE, SUB_TILE = 1024, 512
VMEM_LIMIT = 60 * 1024 * 1024

BF16 = jnp.bfloat16
F32 = jnp.float32


def _rms(xf, gain):
    return xf * lax.rsqrt(jnp.mean(xf * xf, axis=-1, keepdims=True) + EPS) * gain


def _params(n_grid_dims):
    return pltpu.CompilerParams(
        dimension_semantics=("arbitrary",) * n_grid_dims,
        vmem_limit_bytes=VMEM_LIMIT)


def _head0_lanes():
    return lax.broadcasted_iota(jnp.int32, (1, LANES), 1) < HEAD_DIM


MEM_ROWS = 4 * N_MEM


def _mem_kv_body(mem_ref, g_ref, w_ref, o_ref):
    x = mem_ref[...]
    xn = x * lax.rsqrt(jnp.mean(x * x, axis=-1, keepdims=True) + EPS)
    for l in range(DEPTH):
        h = (xn * g_ref[l]).astype(BF16)
        o_ref[l] = jnp.dot(h, w_ref[l], preferred_element_type=F32).astype(BF16)


def _mem_kv(mem, mem_norm, w_mem_kv):
    rows = mem.shape[0] * N_MEM
    return pl.pallas_call(
        _mem_kv_body,
        grid=(rows // MEM_ROWS,),
        in_specs=[
            pl.BlockSpec((MEM_ROWS, D_MODEL), lambda i: (i, 0)),
            pl.BlockSpec((DEPTH, 1, D_MODEL), lambda i: (0, 0, 0)),
            pl.BlockSpec((DEPTH, D_MODEL, 2 * MEM_W), lambda i: (0, 0, 0)),
        ],
        out_specs=pl.BlockSpec((DEPTH, MEM_ROWS, 2 * MEM_W), lambda i: (0, i, 0)),
        out_shape=jax.ShapeDtypeStruct((DEPTH, rows, 2 * MEM_W), BF16),
        compiler_params=_params(1),
        name="mem_kv",
    )(mem.reshape(rows, D_MODEL), mem_norm.reshape(DEPTH, 1, D_MODEL), w_mem_kv)


def _kvm_spec(l):
    return pl.BlockSpec((None, N_MEM, 2 * MEM_W), lambda b, j: (l, b, 0))


def _split_heads(q):
    head0 = _head0_lanes()
    zero = jnp.zeros_like(q)
    return jnp.concatenate([jnp.where(head0, q, zero), jnp.where(head0, zero, q)], axis=0)


def _merge_heads(a):
    t = a.shape[0] // 2
    return jnp.where(_head0_lanes(), a[:t], a[t:])


def _with_ones(v):
    return jnp.concatenate([v, jnp.ones_like(v)], axis=1)


def _memory_attention(zm, kvm):
    q = (zm * Q_SCALE).astype(BF16)
    pairs = []
    for p in range(MEM_W // LANES):
        kp = kvm[:, p * LANES:(p + 1) * LANES]
        vp = kvm[:, MEM_W + p * LANES:MEM_W + (p + 1) * LANES]
        s = lax.dot_general(_split_heads(q[:, p * LANES:(p + 1) * LANES]), kp,
                            (((1,), (1,)), ((), ())), preferred_element_type=F32)
        e = jnp.exp(s - jnp.max(s, axis=-1, keepdims=True)).astype(BF16)
        oe = jnp.dot(e, _with_ones(vp), preferred_element_type=F32)
        pairs.append(_merge_heads(oe[:, :LANES]) / _merge_heads(oe[:, LANES:]))
    return jnp.concatenate(pairs, axis=1)


def _rope_tables_body(pos_ref, freq_ref, cos_ref, sin_ref):
    ang = freq_ref[...] * pos_ref[...].astype(F32)
    c = jnp.cos(ang)
    s = jnp.sin(ang)
    cos_ref[...] = jnp.concatenate([c, c, c, c], axis=0).T
    sin_ref[...] = jnp.concatenate([-s, s, -s, s], axis=0).T


def _rope_tables(positions):
    B, S = positions.shape
    half = HEAD_DIM // 2
    freqs = ROPE_THETA ** (-jnp.arange(half, dtype=F32) / half)
    tab = pl.BlockSpec((None, S, LANES), lambda b: (b, 0, 0))
    return pl.pallas_call(
        _rope_tables_body,
        grid=(B,),
        in_specs=[pl.BlockSpec((None, 1, S), lambda b: (b, 0, 0)),
                  pl.BlockSpec((half, 1), lambda b: (0, 0))],
        out_specs=[tab, tab],
        out_shape=[jax.ShapeDtypeStruct((B, S, LANES), F32)] * 2,
        compiler_params=_params(1),
        name="rope_tables",
    )(positions.reshape(B, 1, S), freqs.reshape(half, 1))


def _rope(t, cos, sin_signed):
    first_half = (lax.broadcasted_iota(jnp.int32, (1, LANES), 1) % HEAD_DIM) < (HEAD_DIM // 2)
    cols = []
    for c in range(t.shape[1] // LANES):
        tc = t[:, c * LANES:(c + 1) * LANES]
        partner = jnp.where(first_half,
                            pltpu.roll(tc, LANES - HEAD_DIM // 2, 1),
                            pltpu.roll(tc, HEAD_DIM // 2, 1))
        cols.append(tc * cos + partner * sin_signed)
    return jnp.concatenate(cols, axis=1)


def _pool_means(zh):
    lane = lax.broadcasted_iota(jnp.int32, (1, LANES), 1)
    low = lane < (POOL_GROUP - LANES)
    s2 = zh + pltpu.roll(zh, 1, 0)
    a = s2[:, LANES:]
    s4 = a + pltpu.roll(a, 2, 0)
    a = s4[:, 2 * LANES:]
    s8 = a + pltpu.roll(a, 4, 0)
    a = s8[:, LANES:]
    s16 = a + pltpu.roll(a, 8, 0)
    h = POOL_HALO
    blk = lambda s, c: s[h:, c * LANES:(c + 1) * LANES]
    cols = [
        blk(s2, 0) * 0.5,
        jnp.where(low, blk(s2, 1) * 0.5, blk(s4, 0) * 0.25),
        blk(s4, 1) * 0.25,
        blk(s8, 0) * 0.125,
        jnp.where(low, blk(s8, 1) * 0.125, blk(s16, 0) * 0.0625),
        blk(s16, 1) * 0.0625,
    ]
    return jnp.concatenate(cols, axis=1)


def _mix_a_rows(x, lo, t0, g0, g1, w_in_ref, w_pool_ref, scale_ref, kvm_ref, w_out_ref, zbuf):
    n_rows = x.shape[0]
    h = _rms(x, g0).astype(BF16)
    z = jnp.dot(h, w_in_ref[...], preferred_element_type=F32)
    u = z[:, :MAIN_W]
    zbuf[POOL_HALO + lo:POOL_HALO + lo + n_rows, :] = u
    mean = _pool_means(zbuf[lo:lo + n_rows + POOL_HALO, :])
    t = lax.broadcasted_iota(jnp.int32, (POOL_HALO, MAIN_W), 0) + (t0 + lo)
    col = lax.broadcasted_iota(jnp.int32, (POOL_HALO, MAIN_W), 1)
    win = jnp.full((POOL_HALO, MAIN_W), POOL_WINDOWS[0], jnp.int32)
    for gi in range(1, len(POOL_WINDOWS)):
        win = jnp.where(col >= gi * POOL_GROUP, POOL_WINDOWS[gi], win)
    fix = win.astype(F32) / jnp.minimum(t + 1, win).astype(F32)
    mean = jnp.concatenate([mean[:POOL_HALO] * fix, mean[POOL_HALO:]], axis=0)

    p = (mean - u).astype(BF16)
    y_main = jnp.dot(p, w_pool_ref[...], preferred_element_type=F32) * scale_ref[...]
    y_mem = _memory_attention(z[:, MAIN_W:], kvm_ref[...])
    y = jnp.concatenate([y_main, y_mem], axis=1).astype(BF16)
    y = jnp.dot(y, w_out_ref[...], preferred_element_type=F32)
    return x + _rms(y, g1)


def _ffn_rows(x, g2, g3, w_gu_ref, w_down_ref):
    h = _rms(x, g2).astype(BF16)
    y = None
    for lo, hi in FF_CHUNKS:
        gate = jnp.dot(h, w_gu_ref[:, lo:hi], preferred_element_type=F32)
        up = jnp.dot(h, w_gu_ref[:, D_FF + lo:D_FF + hi], preferred_element_type=F32)
        act = (gate * jax.nn.sigmoid(gate) * up).astype(BF16)
        part = jnp.dot(act, w_down_ref[lo:hi, :], preferred_element_type=F32)
        y = part if y is None else y + part
    return x + _rms(y, g3)


def _resident(block_shape, index_map):
    return pl.BlockSpec(block_shape, index_map, pipeline_mode=pl.Buffered(1))


def _gain_rows(g_ref):
    return [g_ref[i:i + 1, :] for i in range(g_ref.shape[0])]


def _layer_a_body(x_ref, g_ref, w_in_ref, w_pool_ref, scale_ref, kvm_ref, w_out_ref, w_gu_ref, w_down_ref,
                  o_ref, zbuf):
    T = x_ref.shape[0]
    g0, g1, g2, g3 = _gain_rows(g_ref)

    @pl.when(pl.program_id(1) == 0)
    def _():
        zbuf[0:POOL_HALO, :] = jnp.zeros((POOL_HALO, MAIN_W), F32)

    for i in range(T // A_CHAIN):
        lo = i * A_CHAIN
        x1 = _mix_a_rows(x_ref[lo:lo + A_CHAIN, :], lo, pl.program_id(1) * T, g0, g1,
                         w_in_ref, w_pool_ref, scale_ref, kvm_ref, w_out_ref, zbuf)
        o_ref[lo:lo + A_CHAIN, :] = _ffn_rows(x1, g2, g3, w_gu_ref, w_down_ref)
    zbuf[0:POOL_HALO, :] = zbuf[T:T + POOL_HALO, :]


def _layer_a(x, gains, w_in, w_pool_bd, pool_scale, kvm, w_out, w_gu, w_down, l):
    B, S, _ = x.shape
    T = A_TILE
    tok = pl.BlockSpec((None, T, D_MODEL), lambda b, j: (b, j, 0))
    layer = lambda *shape: _resident((None,) + shape, lambda b, j: (l,) + (0,) * len(shape))
    return pl.pallas_call(
        _layer_a_body,
        grid=(B, S // T),
        in_specs=[
            tok, layer(4, D_MODEL),
            layer(D_MODEL, D_MODEL), layer(MAIN_W, MAIN_W), layer(1, MAIN_W),
            _kvm_spec(l),
            layer(D_MODEL, D_MODEL), layer(D_MODEL, 2 * D_FF), layer(D_FF, D_MODEL),
        ],
        out_specs=tok,
        out_shape=jax.ShapeDtypeStruct(x.shape, F32),
        scratch_shapes=[pltpu.VMEM((T + POOL_HALO, MAIN_W), F32)],
        compiler_params=_params(2),
        name="layer_a",
    )(x, gains, w_in, w_pool_bd, pool_scale, kvm, w_out, w_gu, w_down)


def _dil_shape(B, S, g):
    d = DIL_PATTERNS[g][1]
    return (B, d, S // d, GROUP_W)


def _dil_spec(g, T):
    d = DIL_PATTERNS[g][1]
    return pl.BlockSpec((None, d, T // d, GROUP_W), lambda b, j: (b, 0, j, 0))


def _scatter_by_residue(val, slabs, out_ref, g, lo):
    d = DIL_PATTERNS[g][1]
    H = val.shape[0]
    if d == 1:
        out_ref[0, lo:lo + H, :] = val.astype(BF16)
        return
    for p in range(GROUP_W // LANES):
        slabs[p] = val[:, p * LANES:(p + 1) * LANES]
    for r in range(d):
        for p in range(GROUP_W // LANES):
            out_ref[r, lo // d:(lo + H) // d, p * LANES:(p + 1) * LANES] = (
                slabs[p, pl.ds(r, H // d, stride=d), :].astype(BF16))


def _gather_by_residue(in_ref, slabs, g, lo, H):
    d = DIL_PATTERNS[g][1]
    if d == 1:
        return in_ref[0, lo:lo + H, :].astype(F32)
    for r in range(d):
        for p in range(GROUP_W // LANES):
            slabs[p, pl.ds(r, H // d, stride=d), :] = (
                in_ref[r, lo // d:(lo + H) // d, p * LANES:(p + 1) * LANES].astype(F32))
    return jnp.concatenate([slabs[p] for p in range(GROUP_W // LANES)], axis=1)


def _slab_scratch(n_chains, n, chain_rows):
    return pltpu.VMEM((n_chains, n, GROUP_W // LANES, chain_rows, LANES), F32)


def _shared_kv_body(x_ref, g_ref, w_ref, cos_ref, sin_ref, k0, k1, k2, v0, v1, v2, slabs):
    for i in range(x_ref.shape[0] // SUB_TILE):
        lo = i * SUB_TILE
        rows = slice(lo, lo + SUB_TILE)
        h = _rms(x_ref[rows, :], g_ref[...]).astype(BF16)
        kv = jnp.dot(h, w_ref[...], preferred_element_type=F32)
        k = _rope(kv[:, :MAIN_W], cos_ref[rows, :], sin_ref[rows, :])
        for g, (k_ref, v_ref) in enumerate(((k0, v0), (k1, v1), (k2, v2))):
            _scatter_by_residue(k[:, g * GROUP_W:(g + 1) * GROUP_W], slabs.at[i, 0], k_ref, g, lo)
            _scatter_by_residue(kv[:, MAIN_W + g * GROUP_W:MAIN_W + (g + 1) * GROUP_W], slabs.at[i, 1],
                                v_ref, g, lo)


def _shared_kv(x, kv_norm, w_kv, cos_t, sin_t):
    B, S, _ = x.shape
    T = MIX_TILE
    tab = pl.BlockSpec((None, T, LANES), lambda b, j: (b, j, 0))
    res = pl.pallas_call(
        _shared_kv_body,
        grid=(B, S // T),
        in_specs=[
            pl.BlockSpec((None, T, D_MODEL), lambda b, j: (b, j, 0)),
            pl.BlockSpec((1, D_MODEL), lambda b, j: (0, 0)),
            pl.BlockSpec((D_MODEL, 2 * MAIN_W), lambda b, j: (0, 0)),
            tab, tab,
        ],
        out_specs=[_dil_spec(g, T) for g in range(N_GROUPS)] * 2,
        out_shape=[jax.ShapeDtypeStruct(_dil_shape(B, S, g), BF16) for g in range(N_GROUPS)] * 2,
        scratch_shapes=[_slab_scratch(T // SUB_TILE, 2, SUB_TILE)],
        compiler_params=_params(2),
        name="shared_kv",
    )(x, kv_norm, w_kv, cos_t, sin_t)
    return res[:N_GROUPS], res[N_GROUPS:]


def _mix_b_in_body(x_ref, g0_ref, w_in_ref, cos_ref, sin_ref, kvm_ref, q0, q1, q2, ymem_ref, slabs):
    for i in range(x_ref.shape[0] // SUB_TILE):
        lo = i * SUB_TILE
        rows = slice(lo, lo + SUB_TILE)
        h = _rms(x_ref[rows, :], g0_ref[...]).astype(BF16)
        z = jnp.dot(h, w_in_ref[...], preferred_element_type=F32)
        q = _rope(z[:, :MAIN_W], cos_ref[rows, :], sin_ref[rows, :]) * Q_SCALE
        for g, q_ref in enumerate((q0, q1, q2)):
            _scatter_by_residue(q[:, g * GROUP_W:(g + 1) * GROUP_W], slabs.at[i, 0], q_ref, g, lo)
        ymem_ref[rows, :] = _memory_attention(z[:, MAIN_W:], kvm_ref[...]).astype(BF16)


def _mix_b_in(x, g0, w_in, cos_t, sin_t, kvm, l):
    B, S, _ = x.shape
    T = MIX_TILE
    tab = pl.BlockSpec((None, T, LANES), lambda b, j: (b, j, 0))
    res = pl.pallas_call(
        _mix_b_in_body,
        grid=(B, S // T),
        in_specs=[
            pl.BlockSpec((None, T, D_MODEL), lambda b, j: (b, j, 0)),
            pl.BlockSpec((1, D_MODEL), lambda b, j: (0, 0)),
            pl.BlockSpec((None, D_MODEL, D_MODEL), lambda b, j: (l, 0, 0)),
            tab, tab,
            _kvm_spec(l),
        ],
        out_specs=[_dil_spec(g, T) for g in range(N_GROUPS)]
        + [pl.BlockSpec((None, T, MEM_W), lambda b, j: (b, j, 0))],
        out_shape=[jax.ShapeDtypeStruct(_dil_shape(B, S, g), BF16) for g in range(N_GROUPS)]
        + [jax.ShapeDtypeStruct((B, S, MEM_W), BF16)],
        scratch_shapes=[_slab_scratch(T // SUB_TILE, 1, SUB_TILE)],
        compiler_params=_params(2),
        name="mix_b_in",
    )(x, g0, w_in, cos_t, sin_t, kvm)
    return res[:N_GROUPS], res[N_GROUPS]


def _band_attn_body(q_ref, k_ref, v_ref, o_ref, lse_ref):
    A = ATTN_BLOCK
    n_seq, L, _ = q_ref.shape
    qi = lax.broadcasted_iota(jnp.int32, (2 * A, 2 * A), 0) & (A - 1)
    kj = lax.broadcasted_iota(jnp.int32, (2 * A, 2 * A), 1)
    band = (kj >= qi) & (kj <= qi + A)
    causal = (lax.broadcasted_iota(jnp.int32, (2 * A, A), 1)
              <= (lax.broadcasted_iota(jnp.int32, (2 * A, A), 0) & (A - 1)))
    qi2 = lax.broadcasted_iota(jnp.int32, (4 * A, 2 * A), 0)
    kj2 = lax.broadcasted_iota(jnp.int32, (4 * A, 2 * A), 1)
    causal2 = ((qi2 >= 2 * A) == (kj2 >= A)) & ((kj2 & (A - 1)) <= (qi2 & (A - 1)))

    first = [(r, 0) for r in range(n_seq)]
    units = [(first[i:i + 2], causal2) for i in range(0, n_seq - 1, 2)]
    if n_seq % 2:
        units.append((first[-1:], causal))
    units += [([(r, n)], band) for r in range(n_seq) for n in range(1, L // A)]

    for members, mask in units:
        krows = lambda n: slice(0, A) if n == 0 else slice((n - 1) * A, (n + 1) * A)
        for p in range(GROUP_W // LANES):
            cols = slice(p * LANES, (p + 1) * LANES)
            cat = lambda parts: parts[0] if len(parts) == 1 else jnp.concatenate(parts, axis=0)
            q = cat([_split_heads(q_ref[r, n * A:(n + 1) * A, cols]) for r, n in members])
            k = cat([k_ref[r, krows(n), cols] for r, n in members])
            v = cat([v_ref[r, krows(n), cols] for r, n in members])
            s = lax.dot_general(q, k, (((1,), (1,)), ((), ())), preferred_element_type=F32)
            s = jnp.where(mask, s, NEG)
            m = jnp.max(s, axis=-1, keepdims=True)
            e = jnp.exp(s - m).astype(BF16)
            oe = jnp.dot(e, _with_ones(v), preferred_element_type=F32)
            for i, (r, n) in enumerate(members):
                part = slice(i * 2 * A, (i + 1) * 2 * A)
                den = _merge_heads(oe[part, LANES:])
                o_ref[r, n * A:(n + 1) * A, cols] = (_merge_heads(oe[part, :LANES]) / den).astype(BF16)
                lse_ref[r, n * A:(n + 1) * A, cols] = (
                    _merge_heads(jnp.broadcast_to(m[part], (2 * A, LANES))) + jnp.log(den))


def _band_attn(q, k, v, g):
    B, d, L, _ = q.shape
    assert DIL_PATTERNS[g][0] // d == ATTN_BLOCK and L % ATTN_BLOCK == 0
    spec = pl.BlockSpec((None, d, L, GROUP_W), lambda b: (b, 0, 0, 0))
    return pl.pallas_call(
        _band_attn_body,
        grid=(B,),
        in_specs=[spec, spec, spec],
        out_specs=[spec, spec],
        out_shape=[jax.ShapeDtypeStruct(q.shape, BF16), jax.ShapeDtypeStruct(q.shape, F32)],
        compiler_params=_params(1),
        name=f"band_attn_g{g}",
    )(q, k, v)


def _mix_b_out_rows(x, lo, g1, o_refs, l_refs, ymem_ref, w_out_ref, slabs):
    H = x.shape[0]
    outs = [_gather_by_residue(r, slabs.at[2 * g], g, lo, H) for g, r in enumerate(o_refs)]
    lses = [_gather_by_residue(r, slabs.at[2 * g + 1], g, lo, H) for g, r in enumerate(l_refs)]
    m = jnp.maximum(jnp.maximum(lses[0], lses[1]), lses[2])
    es = [jnp.exp(l - m) for l in lses]
    inv = 1.0 / (es[0] + es[1] + es[2])
    cols = [(outs[g] * (es[g] * inv)).astype(BF16) for g in range(N_GROUPS)]
    y = jnp.concatenate(cols + [ymem_ref[lo:lo + H, :]], axis=1)
    y = jnp.dot(y, w_out_ref[...], preferred_element_type=F32)
    return x + _rms(y, g1)


def _layer_b_body(x_ref, g_ref, o0_ref, o1_ref, o2_ref, l0_ref, l1_ref, l2_ref, ymem_ref,
                  w_out_ref, w_gu_ref, w_down_ref, out_ref, slabs):
    _, g1, g2, g3 = _gain_rows(g_ref)
    for i in range(x_ref.shape[0] // B_CHAIN):
        lo = i * B_CHAIN
        x1 = _mix_b_out_rows(x_ref[lo:lo + B_CHAIN, :], lo, g1, (o0_ref, o1_ref, o2_ref),
                             (l0_ref, l1_ref, l2_ref), ymem_ref, w_out_ref, slabs.at[i])
        out_ref[lo:lo + B_CHAIN, :] = _ffn_rows(x1, g2, g3, w_gu_ref, w_down_ref)


def _layer_b_tail(x, gains, outs, lses, ymem, w_out, w_gu, w_down, l):
    B, S, _ = x.shape
    T = B_TILE
    tok = pl.BlockSpec((None, T, D_MODEL), lambda b, j: (b, j, 0))
    grp = [_dil_spec(g, T) for g in range(N_GROUPS)]
    layer = lambda *shape: _resident((None,) + shape, lambda b, j: (l,) + (0,) * len(shape))
    return pl.pallas_call(
        _layer_b_body,
        grid=(B, S // T),
        in_specs=[
            tok, layer(4, D_MODEL),
            *grp, *grp,
            pl.BlockSpec((None, T, MEM_W), lambda b, j: (b, j, 0)),
            layer(D_MODEL, D_MODEL), layer(D_MODEL, 2 * D_FF), layer(D_FF, D_MODEL),
        ],
        out_specs=tok,
        out_shape=jax.ShapeDtypeStruct(x.shape, F32),
        scratch_shapes=[_slab_scratch(T // B_CHAIN, 2 * N_GROUPS, B_CHAIN)],
        compiler_params=_params(2),
        name="layer_b_tail",
    )(x, gains, *outs, *lses, ymem, w_out, w_gu, w_down)


def _pool_block_diag(w_pool):
    n_layers, n_grp = w_pool.shape[:2]
    out = jnp.zeros((n_layers, MAIN_W, MAIN_W), w_pool.dtype)
    for g in range(n_grp):
        sl = slice(g * POOL_GROUP, (g + 1) * POOL_GROUP)
        out = out.at[:, sl, sl].set(w_pool[:, g])
    return out


@jax.jit
def kernel(x, mem, positions, norm_gains, mem_norm, w_in, w_mem_kv, w_out, w_pool, pool_scale,
           kv_norm, w_kv, w_gate_up, w_down):
    w_in_b = w_in.astype(BF16)
    w_out_b = w_out.astype(BF16)
    w_gu_b = w_gate_up.astype(BF16)
    w_down_b = w_down.astype(BF16)
    w_pool_b = _pool_block_diag(w_pool).astype(BF16)
    pool_scale = pool_scale.reshape(N_A_LAYERS, 1, MAIN_W)

    kvm = _mem_kv(mem, mem_norm, w_mem_kv.astype(BF16))
    cos_t, sin_t = _rope_tables(positions)
    kg = vg = None
    for l in range(DEPTH):
        if l < N_A_LAYERS:
            x = _layer_a(x, norm_gains, w_in_b, w_pool_b, pool_scale, kvm, w_out_b, w_gu_b, w_down_b, l)
        else:
            qg, ymem = _mix_b_in(x, norm_gains[l, 0].reshape(1, D_MODEL), w_in_b, cos_t, sin_t, kvm, l)
            res = [_band_attn(qg[g], kg[g], vg[g], g) for g in range(N_GROUPS)]
            x = _layer_b_tail(x, norm_gains, [r[0] for r in res], [r[1] for r in res], ymem,
                              w_out_b, w_gu_b, w_down_b, l)
        if l == N_A_LAYERS - 1:
            kg, vg = _shared_kv(x, kv_norm.reshape(1, D_MODEL), w_kv.astype(BF16), cos_t, sin_t)
    return x
```

```python
import jax
import jax.numpy as jnp
from jax import lax
from jax.experimental import pallas as pl
from jax.experimental.pallas import tpu as pltpu

D_MODEL = 1024
DEPTH = 4
N_MEM = 256
HEAD_DIM = 64
N_MEM_HEADS = 4
MEM_W = N_MEM_HEADS * HEAD_DIM
MAIN_W = D_MODEL - MEM_W
POOL_WINDOWS = (2, 4, 8, 16)
POOL_GROUP = MAIN_W // len(POOL_WINDOWS)
POOL_HALO = max(POOL_WINDOWS)
DIL_PATTERNS = ((128, 1), (512, 4), (2048, 16))
N_GROUPS = len(DIL_PATTERNS)
GROUP_W = MAIN_W // N_GROUPS
N_A_LAYERS = DEPTH // 2
D_FF = ((8 * D_MODEL + 3 * 256 - 1) // (3 * 256)) * 256
ROPE_THETA = 10000.0
EPS = 1e-6
NEG = -1e30
Q_SCALE = HEAD_DIM ** -0.5

LANES = 128
ATTN_BLOCK = 128
A_TILE, A_CHAIN = 1024, 512
B_TILE, B_CHAIN = 512, 256
MIX_TILE, SUB_TILE = 1024, 512
VMEM_LIMIT = 60 * 1024 * 1024

BF16 = jnp.bfloat16
F32 = jnp.float32


def _rms(xf, gain):
    return xf * lax.rsqrt(jnp.mean(xf * xf, axis=-1, keepdims=True) + EPS) * gain


def _params(n_grid_dims):
    return pltpu.CompilerParams(
        dimension_semantics=("arbitrary",) * n_grid_dims,
        vmem_limit_bytes=VMEM_LIMIT)


def _head0_lanes():
    return lax.broadcasted_iota(jnp.int32, (1, LANES), 1) < HEAD_DIM


MEM_ROWS = 4 * N_MEM


def _mem_kv_body(mem_ref, g_ref, w_ref, o_ref):
    x = mem_ref[...]
    xn = x * lax.rsqrt(jnp.mean(x * x, axis=-1, keepdims=True) + EPS)
    for l in range(DEPTH):
        h = (xn * g_ref[l]).astype(BF16)
        o_ref[l] = jnp.dot(h, w_ref[l], preferred_element_type=F32).astype(BF16)


def _mem_kv(mem, mem_norm, w_mem_kv):
    rows = mem.shape[0] * N_MEM
    return pl.pallas_call(
        _mem_kv_body,
        grid=(rows // MEM_ROWS,),
        in_specs=[
            pl.BlockSpec((MEM_ROWS, D_MODEL), lambda i: (i, 0)),
            pl.BlockSpec((DEPTH, 1, D_MODEL), lambda i: (0, 0, 0)),
            pl.BlockSpec((DEPTH, D_MODEL, 2 * MEM_W), lambda i: (0, 0, 0)),
        ],
        out_specs=pl.BlockSpec((DEPTH, MEM_ROWS, 2 * MEM_W), lambda i: (0, i, 0)),
        out_shape=jax.ShapeDtypeStruct((DEPTH, rows, 2 * MEM_W), BF16),
        compiler_params=_params(1),
        name="mem_kv",
    )(mem.reshape(rows, D_MODEL), mem_norm.reshape(DEPTH, 1, D_MODEL), w_mem_kv)


def _kvm_spec(l):
    return pl.BlockSpec((None, N_MEM, 2 * MEM_W), lambda b, j: (l, b, 0))


def _split_heads(q):
    head0 = _head0_lanes()
    zero = jnp.zeros_like(q)
    return jnp.concatenate([jnp.where(head0, q, zero), jnp.where(head0, zero, q)], axis=0)


def _merge_heads(a):
    t = a.shape[0] // 2
    return jnp.where(_head0_lanes(), a[:t], a[t:])


def _with_ones(v):
    return jnp.concatenate([v, jnp.ones_like(v)], axis=1)


def _memory_attention(zm, kvm):
    q = (zm * Q_SCALE).astype(BF16)
    pairs = []
    for p in range(MEM_W // LANES):
        kp = kvm[:, p * LANES:(p + 1) * LANES]
        vp = kvm[:, MEM_W + p * LANES:MEM_W + (p + 1) * LANES]
        s = lax.dot_general(_split_heads(q[:, p * LANES:(p + 1) * LANES]), kp,
                            (((1,), (1,)), ((), ())), preferred_element_type=F32)
        e = jnp.exp(s - jnp.max(s, axis=-1, keepdims=True)).astype(BF16)
        oe = jnp.dot(e, _with_ones(vp), preferred_element_type=F32)
        pairs.append(_merge_heads(oe[:, :LANES]) / _merge_heads(oe[:, LANES:]))
    return jnp.concatenate(pairs, axis=1)


def _rope_tables_body(pos_ref, freq_ref, cos_ref, sin_ref):
    ang = freq_ref[...] * pos_ref[...].astype(F32)
    c = jnp.cos(ang)
    s = jnp.sin(ang)
    cos_ref[...] = jnp.concatenate([c, c, c, c], axis=0).T
    sin_ref[...] = jnp.concatenate([-s, s, -s, s], axis=0).T


def _rope_tables(positions):
    B, S = positions.shape
    half = HEAD_DIM // 2
    freqs = ROPE_THETA ** (-jnp.arange(half, dtype=F32) / half)
    tab = pl.BlockSpec((None, S, LANES), lambda b: (b, 0, 0))
    return pl.pallas_call(
        _rope_tables_body,
        grid=(B,),
        in_specs=[pl.BlockSpec((None, 1, S), lambda b: (b, 0, 0)),
                  pl.BlockSpec((half, 1), lambda b: (0, 0))],
        out_specs=[tab, tab],
        out_shape=[jax.ShapeDtypeStruct((B, S, LANES), F32)] * 2,
        compiler_params=_params(1),
        name="rope_tables",
    )(positions.reshape(B, 1, S), freqs.reshape(half, 1))


def _rope(t, cos, sin_signed):
    first_half = (lax.broadcasted_iota(jnp.int32, (1, LANES), 1) % HEAD_DIM) < (HEAD_DIM // 2)
    cols = []
    for c in range(t.shape[1] // LANES):
        tc = t[:, c * LANES:(c + 1) * LANES]
        partner = jnp.where(first_half,
                            pltpu.roll(tc, LANES - HEAD_DIM // 2, 1),
                            pltpu.roll(tc, HEAD_DIM // 2, 1))
        cols.append(tc * cos + partner * sin_signed)
    return jnp.concatenate(cols, axis=1)


def _pool_means(zh):
    lane = lax.broadcasted_iota(jnp.int32, (1, LANES), 1)
    low = lane < (POOL_GROUP - LANES)
    s2 = zh + pltpu.roll(zh, 1, 0)
    a = s2[:, LANES:]
    s4 = a + pltpu.roll(a, 2, 0)
    a = s4[:, 2 * LANES:]
    s8 = a + pltpu.roll(a, 4, 0)
    a = s8[:, LANES:]
    s16 = a + pltpu.roll(a, 8, 0)
    h = POOL_HALO
    blk = lambda s, c: s[h:, c * LANES:(c + 1) * LANES]
    cols = [
        blk(s2, 0) * 0.5,
        jnp.where(low, blk(s2, 1) * 0.5, blk(s4, 0) * 0.25),
        blk(s4, 1) * 0.25,
        blk(s8, 0) * 0.125,
        jnp.where(low, blk(s8, 1) * 0.125, blk(s16, 0) * 0.0625),
        blk(s16, 1) * 0.0625,
    ]
    return jnp.concatenate(cols, axis=1)


def _mix_a_rows(x, z, lo, t0, g1, w_pool_ref, scale_ref, kvm_ref, w_out_ref, zbuf):
    n_rows = x.shape[0]
    u = z[:, :MAIN_W]
    zbuf[POOL_HALO + lo:POOL_HALO + lo + n_rows, :] = u
    mean = _pool_means(zbuf[lo:lo + n_rows + POOL_HALO, :])
    t = lax.broadcasted_iota(jnp.int32, (POOL_HALO, MAIN_W), 0) + (t0 + lo)
    col = lax.broadcasted_iota(jnp.int32, (POOL_HALO, MAIN_W), 1)
    win = jnp.full((POOL_HALO, MAIN_W), POOL_WINDOWS[0], jnp.int32)
    for gi in range(1, len(POOL_WINDOWS)):
        win = jnp.where(col >= gi * POOL_GROUP, POOL_WINDOWS[gi], win)
    fix = win.astype(F32) / jnp.minimum(t + 1, win).astype(F32)
    mean = jnp.concatenate([mean[:POOL_HALO] * fix, mean[POOL_HALO:]], axis=0)

    p = (mean - u).astype(BF16)
    y_main = jnp.dot(p, w_pool_ref[...], preferred_element_type=F32) * scale_ref[...]
    y_mem = _memory_attention(z[:, MAIN_W:], kvm_ref[...])
    y = jnp.concatenate([y_main, y_mem], axis=1).astype(BF16)
    y = jnp.dot(y, w_out_ref[...], preferred_element_type=F32)
    return x + _rms(y, g1)


def _ffn_rows(x, g2, g3, w_gu_ref, w_down_ref):
    h = _rms(x, g2).astype(BF16)
    gate = jnp.dot(h, w_gu_ref[:, :D_FF], preferred_element_type=F32)
    up = jnp.dot(h, w_gu_ref[:, D_FF:], preferred_element_type=F32)
    act = (gate * jax.nn.sigmoid(gate) * up).astype(BF16)
    y = jnp.dot(act, w_down_ref[...], preferred_element_type=F32)
    return x + _rms(y, g3)


def _resident(block_shape, index_map):
    return pl.BlockSpec(block_shape, index_map, pipeline_mode=pl.Buffered(1))


def _gain_rows(g_ref):
    return [g_ref[i:i + 1, :] for i in range(g_ref.shape[0])]


def _layer_a_body(x_ref, g_ref, w_in_ref, w_pool_ref, scale_ref, kvm_ref, w_out_ref, w_gu_ref, w_down_ref,
                  o_ref, zbuf):
    T = x_ref.shape[0]
    g0, g1, g2, g3 = _gain_rows(g_ref)

    @pl.when(pl.program_id(1) == 0)
    def _():
        zbuf[0:POOL_HALO, :] = jnp.zeros((POOL_HALO, MAIN_W), F32)

    chains = range(T // A_CHAIN)
    x1 = []
    for i in chains:
        x = x_ref[i * A_CHAIN:(i + 1) * A_CHAIN, :]
        z = jnp.dot(_rms(x, g0).astype(BF16), w_in_ref[...], preferred_element_type=F32)
        x1.append(_mix_a_rows(x, z, i * A_CHAIN, pl.program_id(1) * T, g1,
                              w_pool_ref, scale_ref, kvm_ref, w_out_ref, zbuf))
    zbuf[0:POOL_HALO, :] = zbuf[T:T + POOL_HALO, :]
    for i in chains:
        o_ref[i * A_CHAIN:(i + 1) * A_CHAIN, :] = _ffn_rows(x1[i], g2, g3, w_gu_ref, w_down_ref)


def _layer_a(x, gains, w_in, w_pool_bd, pool_scale, kvm, w_out, w_gu, w_down, l):
    B, S, _ = x.shape
    T = A_TILE
    tok = pl.BlockSpec((None, T, D_MODEL), lambda b, j: (b, j, 0))
    layer = lambda *shape: _resident((None,) + shape, lambda b, j: (l,) + (0,) * len(shape))
    return pl.pallas_call(
        _layer_a_body,
        grid=(B, S // T),
        in_specs=[
            tok, layer(4, D_MODEL),
            layer(D_MODEL, D_MODEL), layer(MAIN_W, MAIN_W), layer(1, MAIN_W),
            _kvm_spec(l),
            layer(D_MODEL, D_MODEL), layer(D_MODEL, 2 * D_FF), layer(D_FF, D_MODEL),
        ],
        out_specs=tok,
        out_shape=jax.ShapeDtypeStruct(x.shape, F32),
        scratch_shapes=[pltpu.VMEM((T + POOL_HALO, MAIN_W), F32)],
        compiler_params=_params(2),
        name="layer_a",
    )(x, gains, w_in, w_pool_bd, pool_scale, kvm, w_out, w_gu, w_down)


def _dil_shape(B, S, g):
    d = DIL_PATTERNS[g][1]
    return (B, d, S // d, GROUP_W)


def _dil_spec(g, T):
    d = DIL_PATTERNS[g][1]
    return pl.BlockSpec((None, d, T // d, GROUP_W), lambda b, j: (b, 0, j, 0))


def _scatter_by_residue(val, slabs, out_ref, g, lo):
    d = DIL_PATTERNS[g][1]
    H = val.shape[0]
    if d == 1:
        out_ref[0, lo:lo + H, :] = val.astype(BF16)
        return
    for p in range(GROUP_W // LANES):
        slabs[p] = val[:, p * LANES:(p + 1) * LANES]
    for r in range(d):
        for p in range(GROUP_W // LANES):
            out_ref[r, lo // d:(lo + H) // d, p * LANES:(p + 1) * LANES] = (
                slabs[p, pl.ds(r, H // d, stride=d), :].astype(BF16))


def _gather_by_residue(in_ref, slabs, g, lo, H):
    d = DIL_PATTERNS[g][1]
    if d == 1:
        return in_ref[0, lo:lo + H, :].astype(F32)
    for r in range(d):
        for p in range(GROUP_W // LANES):
            slabs[p, pl.ds(r, H // d, stride=d), :] = (
                in_ref[r, lo // d:(lo + H) // d, p * LANES:(p + 1) * LANES].astype(F32))
    return jnp.concatenate([slabs[p] for p in range(GROUP_W // LANES)], axis=1)


def _slab_scratch(n_chains, n, chain_rows):
    return pltpu.VMEM((n_chains, n, GROUP_W // LANES, chain_rows, LANES), F32)


def _shared_kv_body(x_ref, g_ref, w_ref, cos_ref, sin_ref, k0, k1, k2, v0, v1, v2, slabs):
    chains = range(x_ref.shape[0] // SUB_TILE)
    rows = [slice(i * SUB_TILE, (i + 1) * SUB_TILE) for i in chains]
    kvs = [jnp.dot(_rms(x_ref[rows[i], :], g_ref[...]).astype(BF16), w_ref[...],
                   preferred_element_type=F32) for i in chains]
    for i in chains:
        kv, lo = kvs[i], i * SUB_TILE
        k = _rope(kv[:, :MAIN_W], cos_ref[rows[i], :], sin_ref[rows[i], :])
        for g, (k_ref, v_ref) in enumerate(((k0, v0), (k1, v1), (k2, v2))):
            _scatter_by_residue(k[:, g * GROUP_W:(g + 1) * GROUP_W], slabs.at[i, 0], k_ref, g, lo)
            _scatter_by_residue(kv[:, MAIN_W + g * GROUP_W:MAIN_W + (g + 1) * GROUP_W], slabs.at[i, 1],
                                v_ref, g, lo)


def _shared_kv(x, kv_norm, w_kv, cos_t, sin_t):
    B, S, _ = x.shape
    T = MIX_TILE
    tab = pl.BlockSpec((None, T, LANES), lambda b, j: (b, j, 0))
    res = pl.pallas_call(
        _shared_kv_body,
        grid=(B, S // T),
        in_specs=[
            pl.BlockSpec((None, T, D_MODEL), lambda b, j: (b, j, 0)),
            pl.BlockSpec((1, D_MODEL), lambda b, j: (0, 0)),
            pl.BlockSpec((D_MODEL, 2 * MAIN_W), lambda b, j: (0, 0)),
            tab, tab,
        ],
        out_specs=[_dil_spec(g, T) for g in range(N_GROUPS)] * 2,
        out_shape=[jax.ShapeDtypeStruct(_dil_shape(B, S, g), BF16) for g in range(N_GROUPS)] * 2,
        scratch_shapes=[_slab_scratch(T // SUB_TILE, 2, SUB_TILE)],
        compiler_params=_params(2),
        name="shared_kv",
    )(x, kv_norm, w_kv, cos_t, sin_t)
    return res[:N_GROUPS], res[N_GROUPS:]


def _mix_b_in_body(x_ref, g0_ref, w_in_ref, cos_ref, sin_ref, kvm_ref, q0, q1, q2, ymem_ref, slabs):
    chains = range(x_ref.shape[0] // SUB_TILE)
    rows = [slice(i * SUB_TILE, (i + 1) * SUB_TILE) for i in chains]
    z = [jnp.dot(_rms(x_ref[rows[i], :], g0_ref[...]).astype(BF16), w_in_ref[...],
                 preferred_element_type=F32) for i in chains]
    for i in chains:
        ymem_ref[rows[i], :] = _memory_attention(z[i][:, MAIN_W:], kvm_ref[...]).astype(BF16)
    for i in chains:
        q = _rope(z[i][:, :MAIN_W], cos_ref[rows[i], :], sin_ref[rows[i], :]) * Q_SCALE
        for g, q_ref in enumerate((q0, q1, q2)):
            _scatter_by_residue(q[:, g * GROUP_W:(g + 1) * GROUP_W], slabs.at[i, 0], q_ref, g, i * SUB_TILE)


def _mix_b_in(x, g0, w_in, cos_t, sin_t, kvm, l):
    B, S, _ = x.shape
    T = MIX_TILE
    tab = pl.BlockSpec((None, T, LANES), lambda b, j: (b, j, 0))
    res = pl.pallas_call(
        _mix_b_in_body,
        grid=(B, S // T),
        in_specs=[
            pl.BlockSpec((None, T, D_MODEL), lambda b, j: (b, j, 0)),
            pl.BlockSpec((1, D_MODEL), lambda b, j: (0, 0)),
            pl.BlockSpec((None, D_MODEL, D_MODEL), lambda b, j: (l, 0, 0)),
            tab, tab,
            _kvm_spec(l),
        ],
        out_specs=[_dil_spec(g, T) for g in range(N_GROUPS)]
        + [pl.BlockSpec((None, T, MEM_W), lambda b, j: (b, j, 0))],
        out_shape=[jax.ShapeDtypeStruct(_dil_shape(B, S, g), BF16) for g in range(N_GROUPS)]
        + [jax.ShapeDtypeStruct((B, S, MEM_W), BF16)],
        scratch_shapes=[_slab_scratch(T // SUB_TILE, 1, SUB_TILE)],
        compiler_params=_params(2),
        name="mix_b_in",
    )(x, g0, w_in, cos_t, sin_t, kvm)
    return res[:N_GROUPS], res[N_GROUPS]


def _band_attn_body(q_ref, k_ref, v_ref, o_ref, lse_ref):
    A = ATTN_BLOCK
    n_seq, L, _ = q_ref.shape
    qi = lax.broadcasted_iota(jnp.int32, (2 * A, 2 * A), 0) & (A - 1)
    kj = lax.broadcasted_iota(jnp.int32, (2 * A, 2 * A), 1)
    band = (kj >= qi) & (kj <= qi + A)
    causal = (lax.broadcasted_iota(jnp.int32, (2 * A, A), 1)
              <= (lax.broadcasted_iota(jnp.int32, (2 * A, A), 0) & (A - 1)))
    qi2 = lax.broadcasted_iota(jnp.int32, (4 * A, 2 * A), 0)
    kj2 = lax.broadcasted_iota(jnp.int32, (4 * A, 2 * A), 1)
    causal2 = ((qi2 >= 2 * A) == (kj2 >= A)) & ((kj2 & (A - 1)) <= (qi2 & (A - 1)))

    first = [(r, 0) for r in range(n_seq)]
    units = [(first[i:i + 2], causal2) for i in range(0, n_seq - 1, 2)]
    if n_seq % 2:
        units.append((first[-1:], causal))
    units += [([(r, n)], band) for r in range(n_seq) for n in range(1, L // A)]

    krows = lambda n: slice(0, A) if n == 0 else slice((n - 1) * A, (n + 1) * A)
    cat = lambda parts: parts[0] if len(parts) == 1 else jnp.concatenate(parts, axis=0)
    for members, mask in units:
        for p in range(GROUP_W // LANES):
            cols = slice(p * LANES, (p + 1) * LANES)
            q = cat([_split_heads(q_ref[r, n * A:(n + 1) * A, cols]) for r, n in members])
            k = cat([k_ref[r, krows(n), cols] for r, n in members])
            v = cat([v_ref[r, krows(n), cols] for r, n in members])
            s = lax.dot_general(q, k, (((1,), (1,)), ((), ())), preferred_element_type=F32)
            s = jnp.where(mask, s, NEG)
            m = jnp.max(s, axis=-1, keepdims=True)
            e = jnp.exp(s - m).astype(BF16)
            oe = jnp.dot(e, _with_ones(v), preferred_element_type=F32)
            for i, (r, n) in enumerate(members):
                part = slice(i * 2 * A, (i + 1) * 2 * A)
                den = _merge_heads(oe[part, LANES:])
                o_ref[r, n * A:(n + 1) * A, cols] = (_merge_heads(oe[part, :LANES]) / den).astype(BF16)
                lse_ref[r, n * A:(n + 1) * A, cols] = (
                    _merge_heads(jnp.broadcast_to(m[part], (2 * A, LANES))) + jnp.log(den))


def _band_attn(q, k, v, g):
    B, d, L, _ = q.shape
    assert DIL_PATTERNS[g][0] // d == ATTN_BLOCK and L % ATTN_BLOCK == 0
    spec = pl.BlockSpec((None, d, L, GROUP_W), lambda b: (b, 0, 0, 0))
    return pl.pallas_call(
        _band_attn_body,
        grid=(B,),
        in_specs=[spec, spec, spec],
        out_specs=[spec, spec],
        out_shape=[jax.ShapeDtypeStruct(q.shape, BF16), jax.ShapeDtypeStruct(q.shape, F32)],
        compiler_params=_params(1),
        name=f"band_attn_g{g}",
    )(q, k, v)


def _mix_b_out_rows(x, lo, g1, o_refs, l_refs, ymem_ref, w_out_ref, slabs):
    H = x.shape[0]
    outs = [_gather_by_residue(r, slabs.at[2 * g], g, lo, H) for g, r in enumerate(o_refs)]
    lses = [_gather_by_residue(r, slabs.at[2 * g + 1], g, lo, H) for g, r in enumerate(l_refs)]
    m = jnp.maximum(jnp.maximum(lses[0], lses[1]), lses[2])
    es = [jnp.exp(l - m) for l in lses]
    inv = 1.0 / (es[0] + es[1] + es[2])
    cols = [(outs[g] * (es[g] * inv)).astype(BF16) for g in range(N_GROUPS)]
    y = jnp.concatenate(cols + [ymem_ref[lo:lo + H, :]], axis=1)
    y = jnp.dot(y, w_out_ref[...], preferred_element_type=F32)
    return x + _rms(y, g1)


def _layer_b_body(x_ref, g_ref, o0_ref, o1_ref, o2_ref, l0_ref, l1_ref, l2_ref, ymem_ref,
                  w_out_ref, w_gu_ref, w_down_ref, out_ref, slabs):
    _, g1, g2, g3 = _gain_rows(g_ref)
    chains = range(x_ref.shape[0] // B_CHAIN)
    x1 = [_mix_b_out_rows(x_ref[i * B_CHAIN:(i + 1) * B_CHAIN, :], i * B_CHAIN, g1, (o0_ref, o1_ref, o2_ref),
                          (l0_ref, l1_ref, l2_ref), ymem_ref, w_out_ref, slabs.at[i]) for i in chains]
    for i in chains:
        out_ref[i * B_CHAIN:(i + 1) * B_CHAIN, :] = _ffn_rows(x1[i], g2, g3, w_gu_ref, w_down_ref)


def _layer_b_tail(x, gains, outs, lses, ymem, w_out, w_gu, w_down, l):
    B, S, _ = x.shape
    T = B_TILE
    tok = pl.BlockSpec((None, T, D_MODEL), lambda b, j: (b, j, 0))
    grp = [_dil_spec(g, T) for g in range(N_GROUPS)]
    layer = lambda *shape: _resident((None,) + shape, lambda b, j: (l,) + (0,) * len(shape))
    return pl.pallas_call(
        _layer_b_body,
        grid=(B, S // T),
        in_specs=[
            tok, layer(4, D_MODEL),
            *grp, *grp,
            pl.BlockSpec((None, T, MEM_W), lambda b, j: (b, j, 0)),
            layer(D_MODEL, D_MODEL), layer(D_MODEL, 2 * D_FF), layer(D_FF, D_MODEL),
        ],
        out_specs=tok,
        out_shape=jax.ShapeDtypeStruct(x.shape, F32),
        scratch_shapes=[_slab_scratch(T // B_CHAIN, 2 * N_GROUPS, B_CHAIN)],
        compiler_params=_params(2),
        name="layer_b_tail",
    )(x, gains, *outs, *lses, ymem, w_out, w_gu, w_down)


def _pool_block_diag(w_pool):
    n_layers, n_grp = w_pool.shape[:2]
    out = jnp.zeros((n_layers, MAIN_W, MAIN_W), w_pool.dtype)
    for g in range(n_grp):
        sl = slice(g * POOL_GROUP, (g + 1) * POOL_GROUP)
        out = out.at[:, sl, sl].set(w_pool[:, g])
    return out


@jax.jit
def kernel(x, mem, positions, norm_gains, mem_norm, w_in, w_mem_kv, w_out, w_pool, pool_scale,
           kv_norm, w_kv, w_gate_up, w_down):
    w_in_b = w_in.astype(BF16)
    w_out_b = w_out.astype(BF16)
    w_gu_b = w_gate_up.astype(BF16)
    w_down_b = w_down.astype(BF16)
    w_pool_b = _pool_block_diag(w_pool).astype(BF16)
    pool_scale = pool_scale.reshape(N_A_LAYERS, 1, MAIN_W)

    kvm = _mem_kv(mem, mem_norm, w_mem_kv.astype(BF16))
    cos_t, sin_t = _rope_tables(positions)
    kg = vg = None
    for l in range(DEPTH):
        if l < N_A_LAYERS:
            x = _layer_a(x, norm_gains, w_in_b, w_pool_b, pool_scale, kvm, w_out_b, w_gu_b, w_down_b, l)
        else:
            qg, ymem = _mix_b_in(x, norm_gains[l, 0].reshape(1, D_MODEL), w_in_b, cos_t, sin_t, kvm, l)
            res = [_band_attn(qg[g], kg[g], vg[g], g) for g in range(N_GROUPS)]
            x = _layer_b_tail(x, norm_gains, [r[0] for r in res], [r[1] for r in res], ymem,
                              w_out_b, w_gu_b, w_down_b, l)
        if l == N_A_LAYERS - 1:
            kg, vg = _shared_kv(x, kv_norm.reshape(1, D_MODEL), w_kv.astype(BF16), cos_t, sin_t)
    return x
```

```python
import functools
import math

import jax
import jax.numpy as jnp
from jax import lax
from jax.experimental import pallas as pl
from jax.experimental.pallas import tpu as pltpu

D_MODEL = 1024
DEPTH = 4
N_MEM = 256
HEAD_DIM = 64
N_MEM_HEADS = 4
MEM_W = N_MEM_HEADS * HEAD_DIM
MAIN_W = D_MODEL - MEM_W
POOL_WINDOWS = (2, 4, 8, 16)
POOL_GROUP = MAIN_W // len(POOL_WINDOWS)
POOL_HALO = max(POOL_WINDOWS)
DIL_PATTERNS = ((128, 1), (512, 4), (2048, 16))
N_GROUPS = len(DIL_PATTERNS)
GROUP_W = MAIN_W // N_GROUPS
N_A_LAYERS = DEPTH // 2
D_FF = ((8 * D_MODEL + 3 * 256 - 1) // (3 * 256)) * 256
ROPE_THETA = 10000.0
EPS = 1e-6
NEG = -1e30
Q_SCALE = HEAD_DIM ** -0.5 * math.log2(math.e)

LANES = 128
ATTN_BLOCK = 128
A_TILE, A_CHAIN = 1024, 512
B_TILE, B_CHAIN = 512, 256
MIX_TILE, SUB_TILE = 1024, 512
VMEM_LIMIT = 60 * 1024 * 1024

BF16 = jnp.bfloat16
F32 = jnp.float32


def _rms(xf, gain):
    return xf * lax.rsqrt(jnp.mean(xf * xf, axis=-1, keepdims=True) + EPS) * gain


def _params(n_grid_dims):
    return pltpu.CompilerParams(
        dimension_semantics=("arbitrary",) * n_grid_dims,
        vmem_limit_bytes=VMEM_LIMIT)


def _head0_lanes():
    return lax.broadcasted_iota(jnp.int32, (1, LANES), 1) < HEAD_DIM


MEM_ROWS = 4 * N_MEM


def _mem_kv_body(mem_ref, g_ref, w_ref, o_ref):
    x = mem_ref[...]
    xn = x * lax.rsqrt(jnp.mean(x * x, axis=-1, keepdims=True) + EPS)
    for l in range(DEPTH):
        h = (xn * g_ref[l]).astype(BF16)
        o_ref[l] = jnp.dot(h, w_ref[l], preferred_element_type=F32).astype(BF16)


def _mem_kv(mem, mem_norm, w_mem_kv):
    rows = mem.shape[0] * N_MEM
    return pl.pallas_call(
        _mem_kv_body,
        grid=(rows // MEM_ROWS,),
        in_specs=[
            pl.BlockSpec((MEM_ROWS, D_MODEL), lambda i: (i, 0)),
            pl.BlockSpec((DEPTH, 1, D_MODEL), lambda i: (0, 0, 0)),
            pl.BlockSpec((DEPTH, D_MODEL, 2 * MEM_W), lambda i: (0, 0, 0)),
        ],
        out_specs=pl.BlockSpec((DEPTH, MEM_ROWS, 2 * MEM_W), lambda i: (0, i, 0)),
        out_shape=jax.ShapeDtypeStruct((DEPTH, rows, 2 * MEM_W), BF16),
        compiler_params=_params(1),
        name="mem_kv",
    )(mem.reshape(rows, D_MODEL), mem_norm.reshape(DEPTH, 1, D_MODEL), w_mem_kv)


def _kvm_spec(l):
    return pl.BlockSpec((None, N_MEM, 2 * MEM_W), lambda b, j: (l, b, 0))


def _split_heads(q):
    head0 = _head0_lanes()
    zero = jnp.zeros_like(q)
    return jnp.concatenate([jnp.where(head0, q, zero), jnp.where(head0, zero, q)], axis=0)


def _merge_heads(a):
    t = a.shape[0] // 2
    return jnp.where(_head0_lanes(), a[:t], a[t:])


def _with_ones(v):
    return jnp.concatenate([v, jnp.ones_like(v)], axis=1)


def _memory_attention(zm, kvm):
    q = (zm * Q_SCALE).astype(BF16)
    pairs = []
    for p in range(MEM_W // LANES):
        kp = kvm[:, p * LANES:(p + 1) * LANES]
        vp = kvm[:, MEM_W + p * LANES:MEM_W + (p + 1) * LANES]
        s = lax.dot_general(_split_heads(q[:, p * LANES:(p + 1) * LANES]), kp,
                            (((1,), (1,)), ((), ())), preferred_element_type=F32)
        e = jnp.exp2(s - jnp.max(s, axis=-1, keepdims=True)).astype(BF16)
        oe = jnp.dot(e, _with_ones(vp), preferred_element_type=F32)
        pairs.append(_merge_heads(oe[:, :LANES]) / _merge_heads(oe[:, LANES:]))
    return jnp.concatenate(pairs, axis=1)


def _rope_tables_body(pos_ref, freq_ref, cos_ref, sin_ref):
    ang = freq_ref[...] * pos_ref[...].astype(F32)
    c = jnp.cos(ang)
    s = jnp.sin(ang)
    cos_ref[...] = jnp.concatenate([c, c, c, c], axis=0).T
    sin_ref[...] = jnp.concatenate([-s, s, -s, s], axis=0).T


def _rope_tables(positions):
    B, S = positions.shape
    half = HEAD_DIM // 2
    freqs = ROPE_THETA ** (-jnp.arange(half, dtype=F32) / half)
    tab = pl.BlockSpec((None, S, LANES), lambda b: (b, 0, 0))
    return pl.pallas_call(
        _rope_tables_body,
        grid=(B,),
        in_specs=[pl.BlockSpec((None, 1, S), lambda b: (b, 0, 0)),
                  pl.BlockSpec((half, 1), lambda b: (0, 0))],
        out_specs=[tab, tab],
        out_shape=[jax.ShapeDtypeStruct((B, S, LANES), F32)] * 2,
        compiler_params=_params(1),
        name="rope_tables",
    )(positions.reshape(B, 1, S), freqs.reshape(half, 1))


def _rope(t, cos, sin_signed):
    first_half = (lax.broadcasted_iota(jnp.int32, (1, LANES), 1) % HEAD_DIM) < (HEAD_DIM // 2)
    cols = []
    for c in range(t.shape[1] // LANES):
        tc = t[:, c * LANES:(c + 1) * LANES]
        partner = jnp.where(first_half,
                            pltpu.roll(tc, LANES - HEAD_DIM // 2, 1),
                            pltpu.roll(tc, HEAD_DIM // 2, 1))
        cols.append(tc * cos + partner * sin_signed)
    return jnp.concatenate(cols, axis=1)


def _pool_means(zh):
    lane = lax.broadcasted_iota(jnp.int32, (1, LANES), 1)
    low = lane < (POOL_GROUP - LANES)
    s2 = zh + pltpu.roll(zh, 1, 0)
    a = s2[:, LANES:]
    s4 = a + pltpu.roll(a, 2, 0)
    a = s4[:, 2 * LANES:]
    s8 = a + pltpu.roll(a, 4, 0)
    a = s8[:, LANES:]
    s16 = a + pltpu.roll(a, 8, 0)
    h = POOL_HALO
    blk = lambda s, c: s[h:, c * LANES:(c + 1) * LANES]
    cols = [
        blk(s2, 0) * 0.5,
        jnp.where(low, blk(s2, 1) * 0.5, blk(s4, 0) * 0.25),
        blk(s4, 1) * 0.25,
        blk(s8, 0) * 0.125,
        jnp.where(low, blk(s8, 1) * 0.125, blk(s16, 0) * 0.0625),
        blk(s16, 1) * 0.0625,
    ]
    return jnp.concatenate(cols, axis=1)


def _mix_a_rows(x, z, lo, t0, g1, w_pool_ref, scale_ref, kvm_ref, w_out_ref, zbuf):
    n_rows = x.shape[0]
    u = z[:, :MAIN_W]
    zbuf[POOL_HALO + lo:POOL_HALO + lo + n_rows, :] = u
    mean = _pool_means(zbuf[lo:lo + n_rows + POOL_HALO, :])
    t = lax.broadcasted_iota(jnp.int32, (POOL_HALO, MAIN_W), 0) + (t0 + lo)
    col = lax.broadcasted_iota(jnp.int32, (POOL_HALO, MAIN_W), 1)
    win = jnp.full((POOL_HALO, MAIN_W), POOL_WINDOWS[0], jnp.int32)
    for gi in range(1, len(POOL_WINDOWS)):
        win = jnp.where(col >= gi * POOL_GROUP, POOL_WINDOWS[gi], win)
    fix = win.astype(F32) / jnp.minimum(t + 1, win).astype(F32)
    mean = jnp.concatenate([mean[:POOL_HALO] * fix, mean[POOL_HALO:]], axis=0)

    p = (mean - u).astype(BF16)
    y_main = jnp.dot(p, w_pool_ref[...], preferred_element_type=F32) * scale_ref[...]
    y_mem = _memory_attention(z[:, MAIN_W:], kvm_ref[...])
    y = jnp.concatenate([y_main, y_mem], axis=1).astype(BF16)
    y = jnp.dot(y, w_out_ref[...], preferred_element_type=F32)
    return x + _rms(y, g1)


def _ffn_rows(x, g2, g3, w_gu_ref, w_down_ref):
    h = _rms(x, g2).astype(BF16)
    gate = jnp.dot(h, w_gu_ref[:, :D_FF], preferred_element_type=F32)
    up = jnp.dot(h, w_gu_ref[:, D_FF:], preferred_element_type=F32)
    act = (gate * jax.nn.sigmoid(gate) * up).astype(BF16)
    y = jnp.dot(act, w_down_ref[...], preferred_element_type=F32)
    return x + _rms(y, g3)


def _resident(block_shape, index_map):
    return pl.BlockSpec(block_shape, index_map, pipeline_mode=pl.Buffered(1))


def _gain_rows(g_ref):
    return [g_ref[i:i + 1, :] for i in range(g_ref.shape[0])]


def _layer_a_body(x_ref, g_ref, w_in_ref, w_pool_ref, scale_ref, kvm_ref, w_out_ref, w_gu_ref, w_down_ref,
                  o_ref, zbuf):
    T = x_ref.shape[0]
    g0, g1, g2, g3 = _gain_rows(g_ref)

    @pl.when(pl.program_id(1) == 0)
    def _():
        zbuf[0:POOL_HALO, :] = jnp.zeros((POOL_HALO, MAIN_W), F32)

    chains = range(T // A_CHAIN)
    x1 = []
    for i in chains:
        x = x_ref[i * A_CHAIN:(i + 1) * A_CHAIN, :]
        z = jnp.dot(_rms(x, g0).astype(BF16), w_in_ref[...], preferred_element_type=F32)
        x1.append(_mix_a_rows(x, z, i * A_CHAIN, pl.program_id(1) * T, g1,
                              w_pool_ref, scale_ref, kvm_ref, w_out_ref, zbuf))
    zbuf[0:POOL_HALO, :] = zbuf[T:T + POOL_HALO, :]
    for i in chains:
        o_ref[i * A_CHAIN:(i + 1) * A_CHAIN, :] = _ffn_rows(x1[i], g2, g3, w_gu_ref, w_down_ref)


def _layer_a(x, gains, w_in, w_pool_bd, pool_scale, kvm, w_out, w_gu, w_down, l):
    B, S, _ = x.shape
    T = A_TILE
    tok = pl.BlockSpec((None, T, D_MODEL), lambda b, j: (b, j, 0))
    layer = lambda *shape: _resident((None,) + shape, lambda b, j: (l,) + (0,) * len(shape))
    return pl.pallas_call(
        _layer_a_body,
        grid=(B, S // T),
        in_specs=[
            tok, layer(4, D_MODEL),
            layer(D_MODEL, D_MODEL), layer(MAIN_W, MAIN_W), layer(1, MAIN_W),
            _kvm_spec(l),
            layer(D_MODEL, D_MODEL), layer(D_MODEL, 2 * D_FF), layer(D_FF, D_MODEL),
        ],
        out_specs=tok,
        out_shape=jax.ShapeDtypeStruct(x.shape, F32),
        scratch_shapes=[pltpu.VMEM((T + POOL_HALO, MAIN_W), F32)],
        compiler_params=_params(2),
        name="layer_a",
    )(x, gains, w_in, w_pool_bd, pool_scale, kvm, w_out, w_gu, w_down)


def _dil_shape(B, S, g):
    d = DIL_PATTERNS[g][1]
    return (B, d, S // d, GROUP_W)


def _dil_spec(g, T):
    d = DIL_PATTERNS[g][1]
    return pl.BlockSpec((None, d, T // d, GROUP_W), lambda b, j: (b, 0, j, 0))


def _scatter_by_residue(val, slabs, out_ref, g, lo):
    d = DIL_PATTERNS[g][1]
    H = val.shape[0]
    if d == 1:
        out_ref[0, lo:lo + H, :] = val.astype(BF16)
        return
    for p in range(GROUP_W // LANES):
        slabs[p] = val[:, p * LANES:(p + 1) * LANES]
    for r in range(d):
        for p in range(GROUP_W // LANES):
            out_ref[r, lo // d:(lo + H) // d, p * LANES:(p + 1) * LANES] = (
                slabs[p, pl.ds(r, H // d, stride=d), :].astype(BF16))


def _gather_by_residue(in_ref, slabs, g, lo, H):
    d = DIL_PATTERNS[g][1]
    if d == 1:
        return in_ref[0, lo:lo + H, :].astype(F32)
    for r in range(d):
        for p in range(GROUP_W // LANES):
            slabs[p, pl.ds(r, H // d, stride=d), :] = (
                in_ref[r, lo // d:(lo + H) // d, p * LANES:(p + 1) * LANES].astype(F32))
    return jnp.concatenate([slabs[p] for p in range(GROUP_W // LANES)], axis=1)


def _slab_scratch(n_chains, n, chain_rows):
    return pltpu.VMEM((n_chains, n, GROUP_W // LANES, chain_rows, LANES), F32)


def _mix_b_in_rows(z, lo, cos_ref, sin_ref, kvm_ref, q_refs, ymem_ref, slabs):
    rows = slice(lo, lo + z.shape[0])
    ymem_ref[rows, :] = _memory_attention(z[:, MAIN_W:], kvm_ref[...]).astype(BF16)
    q = _rope(z[:, :MAIN_W], cos_ref[rows, :], sin_ref[rows, :]) * Q_SCALE
    for g, q_ref in enumerate(q_refs):
        _scatter_by_residue(q[:, g * GROUP_W:(g + 1) * GROUP_W], slabs, q_ref, g, lo)


def _mix_b_in_body(*refs, with_kv):
    x_ref, g0_ref, w_in_ref, cos_ref, sin_ref, kvm_ref = refs[:6]
    n_in = 8 if with_kv else 6
    q_refs, ymem_ref, slabs = refs[n_in:n_in + 3], refs[n_in + 3], refs[-1]
    chains = range(x_ref.shape[0] // SUB_TILE)
    rows = [slice(i * SUB_TILE, (i + 1) * SUB_TILE) for i in chains]
    xn = []
    for i in chains:
        x = x_ref[rows[i], :]
        xn.append(x * lax.rsqrt(jnp.mean(x * x, axis=-1, keepdims=True) + EPS))
    z = [jnp.dot((xn[i] * g0_ref[...]).astype(BF16), w_in_ref[...], preferred_element_type=F32) for i in chains]
    if with_kv:
        kv_norm_ref, w_kv_ref = refs[6:8]
        k_refs, v_refs = refs[n_in + 4:n_in + 7], refs[n_in + 7:n_in + 10]
        kvs = [jnp.dot((xn[i] * kv_norm_ref[...]).astype(BF16), w_kv_ref[...], preferred_element_type=F32)
               for i in chains]
    for i in chains:
        _mix_b_in_rows(z[i], i * SUB_TILE, cos_ref, sin_ref, kvm_ref, q_refs, ymem_ref, slabs.at[i, 0])
    if with_kv:
        for i in chains:
            kv, lo = kvs[i], i * SUB_TILE
            k = _rope(kv[:, :MAIN_W], cos_ref[rows[i], :], sin_ref[rows[i], :])
            for g in range(N_GROUPS):
                _scatter_by_residue(k[:, g * GROUP_W:(g + 1) * GROUP_W], slabs.at[i, 1], k_refs[g], g, lo)
                _scatter_by_residue(kv[:, MAIN_W + g * GROUP_W:MAIN_W + (g + 1) * GROUP_W], slabs.at[i, 2],
                                    v_refs[g], g, lo)


def _mix_b_in(x, g0, w_in, cos_t, sin_t, kvm, l, kv=None):
    B, S, _ = x.shape
    T = MIX_TILE
    tab = pl.BlockSpec((None, T, LANES), lambda b, j: (b, j, 0))
    vec = pl.BlockSpec((1, D_MODEL), lambda b, j: (0, 0))
    grp_specs = [_dil_spec(g, T) for g in range(N_GROUPS)]
    grp_shapes = [jax.ShapeDtypeStruct(_dil_shape(B, S, g), BF16) for g in range(N_GROUPS)]
    in_specs = [
        pl.BlockSpec((None, T, D_MODEL), lambda b, j: (b, j, 0)), vec,
        pl.BlockSpec((None, D_MODEL, D_MODEL), lambda b, j: (l, 0, 0)),
        tab, tab,
        _kvm_spec(l),
    ]
    args = [x, g0, w_in, cos_t, sin_t, kvm]
    out_specs = grp_specs + [pl.BlockSpec((None, T, MEM_W), lambda b, j: (b, j, 0))]
    out_shape = grp_shapes + [jax.ShapeDtypeStruct((B, S, MEM_W), BF16)]
    if kv is not None:
        in_specs += [vec, pl.BlockSpec((D_MODEL, 2 * MAIN_W), lambda b, j: (0, 0))]
        args += list(kv)
        out_specs, out_shape = out_specs + grp_specs * 2, out_shape + grp_shapes * 2
    res = pl.pallas_call(
        functools.partial(_mix_b_in_body, with_kv=kv is not None),
        grid=(B, S // T),
        in_specs=in_specs,
        out_specs=out_specs,
        out_shape=out_shape,
        scratch_shapes=[_slab_scratch(T // SUB_TILE, 3 if kv is not None else 1, SUB_TILE)],
        compiler_params=_params(2),
        name="mix_b_in",
    )(*args)
    n = N_GROUPS
    q_ymem = (res[:n], res[n])
    return (q_ymem, (res[n + 1:2 * n + 1], res[2 * n + 1:])) if kv is not None else q_ymem


def _band_attn_body(q_ref, k_ref, v_ref, o_ref, lse_ref):
    A = ATTN_BLOCK
    n_seq, L, _ = q_ref.shape
    qi = lax.broadcasted_iota(jnp.int32, (2 * A, 2 * A), 0) & (A - 1)
    kj = lax.broadcasted_iota(jnp.int32, (2 * A, 2 * A), 1)
    band = (kj >= qi) & (kj <= qi + A)
    causal = (lax.broadcasted_iota(jnp.int32, (2 * A, A), 1)
              <= (lax.broadcasted_iota(jnp.int32, (2 * A, A), 0) & (A - 1)))
    qi2 = lax.broadcasted_iota(jnp.int32, (4 * A, 2 * A), 0)
    kj2 = lax.broadcasted_iota(jnp.int32, (4 * A, 2 * A), 1)
    causal2 = ((qi2 >= 2 * A) == (kj2 >= A)) & ((kj2 & (A - 1)) <= (qi2 & (A - 1)))

    first = [(r, 0) for r in range(n_seq)]
    units = [(first[i:i + 2], causal2) for i in range(0, n_seq - 1, 2)]
    if n_seq % 2:
        units.append((first[-1:], causal))
    units += [([(r, n)], band) for r in range(n_seq) for n in range(1, L // A)]

    krows = lambda n: slice(0, A) if n == 0 else slice((n - 1) * A, (n + 1) * A)
    cat = lambda parts: parts[0] if len(parts) == 1 else jnp.concatenate(parts, axis=0)
    for members, mask in units:
        for p in range(GROUP_W // LANES):
            cols = slice(p * LANES, (p + 1) * LANES)
            q = cat([_split_heads(q_ref[r, n * A:(n + 1) * A, cols]) for r, n in members])
            k = cat([k_ref[r, krows(n), cols] for r, n in members])
            v = cat([v_ref[r, krows(n), cols] for r, n in members])
            s = lax.dot_general(q, k, (((1,), (1,)), ((), ())), preferred_element_type=F32)
            s = jnp.where(mask, s, NEG)
            m = jnp.max(s, axis=-1, keepdims=True)
            e = jnp.exp2(s - m).astype(BF16)
            oe = jnp.dot(e, _with_ones(v), preferred_element_type=F32)
            for i, (r, n) in enumerate(members):
                part = slice(i * 2 * A, (i + 1) * 2 * A)
                den = _merge_heads(oe[part, LANES:])
                o_ref[r, n * A:(n + 1) * A, cols] = (_merge_heads(oe[part, :LANES]) / den).astype(BF16)
                lse_ref[r, n * A:(n + 1) * A, cols] = (
                    _merge_heads(jnp.broadcast_to(m[part], (2 * A, LANES))) + jnp.log2(den))


def _band_attn(q, k, v, g):
    B, d, L, _ = q.shape
    assert DIL_PATTERNS[g][0] // d == ATTN_BLOCK and L % ATTN_BLOCK == 0
    spec = pl.BlockSpec((None, d, L, GROUP_W), lambda b: (b, 0, 0, 0))
    return pl.pallas_call(
        _band_attn_body,
        grid=(B,),
        in_specs=[spec, spec, spec],
        out_specs=[spec, spec],
        out_shape=[jax.ShapeDtypeStruct(q.shape, BF16), jax.ShapeDtypeStruct(q.shape, F32)],
        compiler_params=_params(1),
        name=f"band_attn_g{g}",
    )(q, k, v)


def _mix_b_out_rows(x, lo, g1, o_refs, l_refs, ymem_ref, w_out_ref, slabs):
    H = x.shape[0]
    outs = [_gather_by_residue(r, slabs.at[2 * g], g, lo, H) for g, r in enumerate(o_refs)]
    lses = [_gather_by_residue(r, slabs.at[2 * g + 1], g, lo, H) for g, r in enumerate(l_refs)]
    m = jnp.maximum(jnp.maximum(lses[0], lses[1]), lses[2])
    es = [jnp.exp2(l - m) for l in lses]
    inv = 1.0 / (es[0] + es[1] + es[2])
    cols = [(outs[g] * (es[g] * inv)).astype(BF16) for g in range(N_GROUPS)]
    y = jnp.concatenate(cols + [ymem_ref[lo:lo + H, :]], axis=1)
    y = jnp.dot(y, w_out_ref[...], preferred_element_type=F32)
    return x + _rms(y, g1)


def _layer_b_body(x_ref, g_ref, o0_ref, o1_ref, o2_ref, l0_ref, l1_ref, l2_ref, ymem_ref,
                  w_out_ref, w_gu_ref, w_down_ref, out_ref, slabs):
    _, g1, g2, g3 = _gain_rows(g_ref)
    chains = range(x_ref.shape[0] // B_CHAIN)
    x1 = [_mix_b_out_rows(x_ref[i * B_CHAIN:(i + 1) * B_CHAIN, :], i * B_CHAIN, g1, (o0_ref, o1_ref, o2_ref),
                          (l0_ref, l1_ref, l2_ref), ymem_ref, w_out_ref, slabs.at[i]) for i in chains]
    for i in chains:
        out_ref[i * B_CHAIN:(i + 1) * B_CHAIN, :] = _ffn_rows(x1[i], g2, g3, w_gu_ref, w_down_ref)


def _layer_b_tail(x, gains, outs, lses, ymem, w_out, w_gu, w_down, l):
    B, S, _ = x.shape
    T = B_TILE
    tok = pl.BlockSpec((None, T, D_MODEL), lambda b, j: (b, j, 0))
    grp = [_dil_spec(g, T) for g in range(N_GROUPS)]
    layer = lambda *shape: _resident((None,) + shape, lambda b, j: (l,) + (0,) * len(shape))
    return pl.pallas_call(
        _layer_b_body,
        grid=(B, S // T),
        in_specs=[
            tok, layer(4, D_MODEL),
            *grp, *grp,
            pl.BlockSpec((None, T, MEM_W), lambda b, j: (b, j, 0)),
            layer(D_MODEL, D_MODEL), layer(D_MODEL, 2 * D_FF), layer(D_FF, D_MODEL),
        ],
        out_specs=tok,
        out_shape=jax.ShapeDtypeStruct(x.shape, F32),
        scratch_shapes=[_slab_scratch(T // B_CHAIN, 2 * N_GROUPS, B_CHAIN)],
        compiler_params=_params(2),
        name="layer_b_tail",
    )(x, gains, *outs, *lses, ymem, w_out, w_gu, w_down)


def _pool_block_diag(w_pool):
    n_layers, n_grp = w_pool.shape[:2]
    out = jnp.zeros((n_layers, MAIN_W, MAIN_W), w_pool.dtype)
    for g in range(n_grp):
        sl = slice(g * POOL_GROUP, (g + 1) * POOL_GROUP)
        out = out.at[:, sl, sl].set(w_pool[:, g])
    return out


@jax.jit
def kernel(x, mem, positions, norm_gains, mem_norm, w_in, w_mem_kv, w_out, w_pool, pool_scale,
           kv_norm, w_kv, w_gate_up, w_down):
    w_in_b = w_in.astype(BF16)
    w_out_b = w_out.astype(BF16)
    w_gu_b = w_gate_up.astype(BF16)
    w_down_b = w_down.astype(BF16)
    w_pool_b = _pool_block_diag(w_pool).astype(BF16)
    pool_scale = pool_scale.reshape(N_A_LAYERS, 1, MAIN_W)

    kvm = _mem_kv(mem, mem_norm, w_mem_kv.astype(BF16))
    cos_t, sin_t = _rope_tables(positions)
    kg = vg = None
    for l in range(DEPTH):
        if l < N_A_LAYERS:
            x = _layer_a(x, norm_gains, w_in_b, w_pool_b, pool_scale, kvm, w_out_b, w_gu_b, w_down_b, l)
            continue
        g0 = norm_gains[l, 0].reshape(1, D_MODEL)
        if l == N_A_LAYERS:
            (qg, ymem), (kg, vg) = _mix_b_in(x, g0, w_in_b, cos_t, sin_t, kvm, l,
                                             kv=(kv_norm.reshape(1, D_MODEL), w_kv.astype(BF16)))
        else:
            qg, ymem = _mix_b_in(x, g0, w_in_b, cos_t, sin_t, kvm, l)
        res = [_band_attn(qg[g], kg[g], vg[g], g) for g in range(N_GROUPS)]
        x = _layer_b_tail(x, norm_gains, [r[0] for r in res], [r[1] for r in res], ymem,
                          w_out_b, w_gu_b, w_down_b, l)
    return x
```

```python
import functools
import math

import jax
import jax.numpy as jnp
from jax import lax
from jax.experimental import pallas as pl
from jax.experimental.pallas import tpu as pltpu

D_MODEL = 1024
DEPTH = 4
N_MEM = 256
HEAD_DIM = 64
N_MEM_HEADS = 4
MEM_W = N_MEM_HEADS * HEAD_DIM
MAIN_W = D_MODEL - MEM_W
POOL_WINDOWS = (2, 4, 8, 16)
POOL_GROUP = MAIN_W // len(POOL_WINDOWS)
POOL_HALO = max(POOL_WINDOWS)
DIL_PATTERNS = ((128, 1), (512, 4), (2048, 16))
N_GROUPS = len(DIL_PATTERNS)
GROUP_W = MAIN_W // N_GROUPS
N_A_LAYERS = DEPTH // 2
D_FF = ((8 * D_MODEL + 3 * 256 - 1) // (3 * 256)) * 256
ROPE_THETA = 10000.0
EPS = 1e-6
NEG = -1e30
Q_SCALE = HEAD_DIM ** -0.5 * math.log2(math.e)

LANES = 128
ATTN_BLOCK = 128
A_TILE, A_CHAIN = 1024, 512
B_TILE, B_CHAIN = 512, 256
MIX_TILE, SUB_TILE = 1024, 512
VMEM_LIMIT = 60 * 1024 * 1024

BF16 = jnp.bfloat16
F32 = jnp.float32


def _rms(xf, gain):
    return xf * lax.rsqrt(jnp.mean(xf * xf, axis=-1, keepdims=True) + EPS) * gain


def _params(n_grid_dims):
    return pltpu.CompilerParams(
        dimension_semantics=("arbitrary",) * n_grid_dims,
        vmem_limit_bytes=VMEM_LIMIT)


def _head0_lanes():
    return lax.broadcasted_iota(jnp.int32, (1, LANES), 1) < HEAD_DIM


MEM_ROWS = 4 * N_MEM


def _mem_kv_body(mem_ref, g_ref, w_ref, o_ref):
    x = mem_ref[...]
    xn = x * lax.rsqrt(jnp.mean(x * x, axis=-1, keepdims=True) + EPS)
    for l in range(DEPTH):
        h = (xn * g_ref[l]).astype(BF16)
        o_ref[l] = jnp.dot(h, w_ref[l], preferred_element_type=F32).astype(BF16)


def _mem_kv(mem, mem_norm, w_mem_kv):
    rows = mem.shape[0] * N_MEM
    return pl.pallas_call(
        _mem_kv_body,
        grid=(rows // MEM_ROWS,),
        in_specs=[
            pl.BlockSpec((MEM_ROWS, D_MODEL), lambda i: (i, 0)),
            pl.BlockSpec((DEPTH, 1, D_MODEL), lambda i: (0, 0, 0)),
            pl.BlockSpec((DEPTH, D_MODEL, 2 * MEM_W), lambda i: (0, 0, 0)),
        ],
        out_specs=pl.BlockSpec((DEPTH, MEM_ROWS, 2 * MEM_W), lambda i: (0, i, 0)),
        out_shape=jax.ShapeDtypeStruct((DEPTH, rows, 2 * MEM_W), BF16),
        compiler_params=_params(1),
        name="mem_kv",
    )(mem.reshape(rows, D_MODEL), mem_norm.reshape(DEPTH, 1, D_MODEL), w_mem_kv)


def _kvm_spec(l):
    return pl.BlockSpec((None, N_MEM, 2 * MEM_W), lambda b, j: (l, b, 0))


def _split_heads(q):
    head0 = _head0_lanes()
    zero = jnp.zeros_like(q)
    return jnp.concatenate([jnp.where(head0, q, zero), jnp.where(head0, zero, q)], axis=0)


def _merge_heads(a):
    t = a.shape[0] // 2
    return jnp.where(_head0_lanes(), a[:t], a[t:])


def _with_ones(v):
    return jnp.concatenate([v, jnp.ones_like(v)], axis=1)


def _memory_attention(zm, kvm):
    q = (zm * Q_SCALE).astype(BF16)
    pairs = []
    for p in range(MEM_W // LANES):
        kp = kvm[:, p * LANES:(p + 1) * LANES]
        vp = kvm[:, MEM_W + p * LANES:MEM_W + (p + 1) * LANES]
        s = lax.dot_general(_split_heads(q[:, p * LANES:(p + 1) * LANES]), kp,
                            (((1,), (1,)), ((), ())), preferred_element_type=F32)
        e = jnp.exp2(s - jnp.max(s, axis=-1, keepdims=True)).astype(BF16)
        oe = jnp.dot(e, _with_ones(vp), preferred_element_type=F32)
        pairs.append(_merge_heads(oe[:, :LANES]) / _merge_heads(oe[:, LANES:]))
    return jnp.concatenate(pairs, axis=1)


def _rope_tables_body(pos_ref, freq_ref, cos_ref, sin_ref):
    ang = freq_ref[...] * pos_ref[...].astype(F32)
    c = jnp.cos(ang)
    s = jnp.sin(ang)
    cos_ref[...] = jnp.concatenate([c, c, c, c], axis=0).T
    sin_ref[...] = jnp.concatenate([-s, s, -s, s], axis=0).T


def _rope_tables(positions):
    B, S = positions.shape
    half = HEAD_DIM // 2
    freqs = ROPE_THETA ** (-jnp.arange(half, dtype=F32) / half)
    tab = pl.BlockSpec((None, S, LANES), lambda b: (b, 0, 0))
    return pl.pallas_call(
        _rope_tables_body,
        grid=(B,),
        in_specs=[pl.BlockSpec((None, 1, S), lambda b: (b, 0, 0)),
                  pl.BlockSpec((half, 1), lambda b: (0, 0))],
        out_specs=[tab, tab],
        out_shape=[jax.ShapeDtypeStruct((B, S, LANES), F32)] * 2,
        compiler_params=_params(1),
        name="rope_tables",
    )(positions.reshape(B, 1, S), freqs.reshape(half, 1))


def _rope(t, cos, sin_signed):
    first_half = (lax.broadcasted_iota(jnp.int32, (1, LANES), 1) % HEAD_DIM) < (HEAD_DIM // 2)
    cols = []
    for c in range(t.shape[1] // LANES):
        tc = t[:, c * LANES:(c + 1) * LANES]
        partner = jnp.where(first_half,
                            pltpu.roll(tc, LANES - HEAD_DIM // 2, 1),
                            pltpu.roll(tc, HEAD_DIM // 2, 1))
        cols.append(tc * cos + partner * sin_signed)
    return jnp.concatenate(cols, axis=1)


def _pool_means(zh):
    lane = lax.broadcasted_iota(jnp.int32, (1, LANES), 1)
    low = lane < (POOL_GROUP - LANES)
    s2 = zh + pltpu.roll(zh, 1, 0)
    a = s2[:, LANES:]
    s4 = a + pltpu.roll(a, 2, 0)
    a = s4[:, 2 * LANES:]
    s8 = a + pltpu.roll(a, 4, 0)
    a = s8[:, LANES:]
    s16 = a + pltpu.roll(a, 8, 0)
    h = POOL_HALO
    blk = lambda s, c: s[h:, c * LANES:(c + 1) * LANES]
    cols = [
        blk(s2, 0) * 0.5,
        jnp.where(low, blk(s2, 1) * 0.5, blk(s4, 0) * 0.25),
        blk(s4, 1) * 0.25,
        blk(s8, 0) * 0.125,
        jnp.where(low, blk(s8, 1) * 0.125, blk(s16, 0) * 0.0625),
        blk(s16, 1) * 0.0625,
    ]
    return jnp.concatenate(cols, axis=1)


def _mix_a_rows(x, z, lo, t0, g1, w_pool_ref, scale_ref, kvm_ref, w_out_ref, zbuf):
    n_rows = x.shape[0]
    u = z[:, :MAIN_W]
    zbuf[POOL_HALO + lo:POOL_HALO + lo + n_rows, :] = u
    mean = _pool_means(zbuf[lo:lo + n_rows + POOL_HALO, :])
    t = lax.broadcasted_iota(jnp.int32, (POOL_HALO, MAIN_W), 0) + (t0 + lo)
    col = lax.broadcasted_iota(jnp.int32, (POOL_HALO, MAIN_W), 1)
    win = jnp.full((POOL_HALO, MAIN_W), POOL_WINDOWS[0], jnp.int32)
    for gi in range(1, len(POOL_WINDOWS)):
        win = jnp.where(col >= gi * POOL_GROUP, POOL_WINDOWS[gi], win)
    fix = win.astype(F32) / jnp.minimum(t + 1, win).astype(F32)
    mean = jnp.concatenate([mean[:POOL_HALO] * fix, mean[POOL_HALO:]], axis=0)

    p = (mean - u).astype(BF16)
    y_main = jnp.dot(p, w_pool_ref[...], preferred_element_type=F32) * scale_ref[...]
    y_mem = _memory_attention(z[:, MAIN_W:], kvm_ref[...])
    y = jnp.concatenate([y_main, y_mem], axis=1).astype(BF16)
    y = jnp.dot(y, w_out_ref[...], preferred_element_type=F32)
    return x + _rms(y, g1)


def _ffn_rows(x, g2, g3, w_gu_ref, w_down_ref):
    h = _rms(x, g2).astype(BF16)
    gate = jnp.dot(h, w_gu_ref[:, :D_FF], preferred_element_type=F32)
    up = jnp.dot(h, w_gu_ref[:, D_FF:], preferred_element_type=F32)
    act = (gate * jax.nn.sigmoid(gate) * up).astype(BF16)
    y = jnp.dot(act, w_down_ref[...], preferred_element_type=F32)
    return x + _rms(y, g3)


def _resident(block_shape, index_map):
    return pl.BlockSpec(block_shape, index_map, pipeline_mode=pl.Buffered(1))


def _gain_rows(g_ref):
    return [g_ref[i:i + 1, :] for i in range(g_ref.shape[0])]


def _layer_a_body(x_ref, g_ref, w_in_ref, w_pool_ref, scale_ref, kvm_ref, w_out_ref, w_gu_ref, w_down_ref,
                  o_ref, zbuf):
    T = x_ref.shape[0]
    g0, g1, g2, g3 = _gain_rows(g_ref)

    @pl.when(pl.program_id(1) == 0)
    def _():
        zbuf[0:POOL_HALO, :] = jnp.zeros((POOL_HALO, MAIN_W), F32)

    chains = range(T // A_CHAIN)
    x1 = []
    for i in chains:
        x = x_ref[i * A_CHAIN:(i + 1) * A_CHAIN, :]
        z = jnp.dot(_rms(x, g0).astype(BF16), w_in_ref[...], preferred_element_type=F32)
        x1.append(_mix_a_rows(x, z, i * A_CHAIN, pl.program_id(1) * T, g1,
                              w_pool_ref, scale_ref, kvm_ref, w_out_ref, zbuf))
    zbuf[0:POOL_HALO, :] = zbuf[T:T + POOL_HALO, :]
    for i in chains:
        o_ref[i * A_CHAIN:(i + 1) * A_CHAIN, :] = _ffn_rows(x1[i], g2, g3, w_gu_ref, w_down_ref)


def _layer_a(x, gains, w_in, w_pool_bd, pool_scale, kvm, w_out, w_gu, w_down, l):
    B, S, _ = x.shape
    T = A_TILE
    tok = pl.BlockSpec((None, T, D_MODEL), lambda b, j: (b, j, 0))
    layer = lambda *shape: _resident((None,) + shape, lambda b, j: (l,) + (0,) * len(shape))
    return pl.pallas_call(
        _layer_a_body,
        grid=(B, S // T),
        in_specs=[
            tok, layer(4, D_MODEL),
            layer(D_MODEL, D_MODEL), layer(MAIN_W, MAIN_W), layer(1, MAIN_W),
            _kvm_spec(l),
            layer(D_MODEL, D_MODEL), layer(D_MODEL, 2 * D_FF), layer(D_FF, D_MODEL),
        ],
        out_specs=tok,
        out_shape=jax.ShapeDtypeStruct(x.shape, F32),
        scratch_shapes=[pltpu.VMEM((T + POOL_HALO, MAIN_W), F32)],
        compiler_params=_params(2),
        name="layer_a",
    )(x, gains, w_in, w_pool_bd, pool_scale, kvm, w_out, w_gu, w_down)


def _dil_shape(B, S, g):
    d = DIL_PATTERNS[g][1]
    return (B, d, S // d, GROUP_W)


def _dil_spec(g, T):
    d = DIL_PATTERNS[g][1]
    return pl.BlockSpec((None, d, T // d, GROUP_W), lambda b, j: (b, 0, j, 0))


def _scatter_by_residue(val, slabs, out_ref, g, lo):
    d = DIL_PATTERNS[g][1]
    H = val.shape[0]
    if d == 1:
        out_ref[0, lo:lo + H, :] = val.astype(BF16)
        return
    for p in range(GROUP_W // LANES):
        slabs[p] = val[:, p * LANES:(p + 1) * LANES]
    for r in range(d):
        for p in range(GROUP_W // LANES):
            out_ref[r, lo // d:(lo + H) // d, p * LANES:(p + 1) * LANES] = (
                slabs[p, pl.ds(r, H // d, stride=d), :].astype(BF16))


def _gather_by_residue(in_ref, slabs, g, lo, H):
    d = DIL_PATTERNS[g][1]
    if d == 1:
        return in_ref[0, lo:lo + H, :].astype(F32)
    for r in range(d):
        for p in range(GROUP_W // LANES):
            slabs[p, pl.ds(r, H // d, stride=d), :] = (
                in_ref[r, lo // d:(lo + H) // d, p * LANES:(p + 1) * LANES].astype(F32))
    return jnp.concatenate([slabs[p] for p in range(GROUP_W // LANES)], axis=1)


def _slab_scratch(n_chains, n, chain_rows):
    return pltpu.VMEM((n_chains, n, GROUP_W // LANES, chain_rows, LANES), F32)


def _mix_b_in_rows(z, lo, cos_ref, sin_ref, kvm_ref, q_refs, ymem_ref, slabs):
    rows = slice(lo, lo + z.shape[0])
    ymem_ref[rows, :] = _memory_attention(z[:, MAIN_W:], kvm_ref[...]).astype(BF16)
    q = _rope(z[:, :MAIN_W], cos_ref[rows, :], sin_ref[rows, :]) * Q_SCALE
    for g, q_ref in enumerate(q_refs):
        _scatter_by_residue(q[:, g * GROUP_W:(g + 1) * GROUP_W], slabs, q_ref, g, lo)


def _mix_b_in_body(*refs, with_kv):
    x_ref, g0_ref, w_in_ref, cos_ref, sin_ref, kvm_ref = refs[:6]
    n_in = 8 if with_kv else 6
    q_refs, ymem_ref, slabs = refs[n_in:n_in + 3], refs[n_in + 3], refs[-1]
    chains = range(x_ref.shape[0] // SUB_TILE)
    rows = [slice(i * SUB_TILE, (i + 1) * SUB_TILE) for i in chains]
    xn = []
    for i in chains:
        x = x_ref[rows[i], :]
        xn.append(x * lax.rsqrt(jnp.mean(x * x, axis=-1, keepdims=True) + EPS))
    z = [jnp.dot((xn[i] * g0_ref[...]).astype(BF16), w_in_ref[...], preferred_element_type=F32) for i in chains]
    if with_kv:
        kv_norm_ref, w_kv_ref = refs[6:8]
        k_refs, v_refs = refs[n_in + 4:n_in + 7], refs[n_in + 7:n_in + 10]
        kvs = [jnp.dot((xn[i] * kv_norm_ref[...]).astype(BF16), w_kv_ref[...], preferred_element_type=F32)
               for i in chains]
    for i in chains:
        _mix_b_in_rows(z[i], i * SUB_TILE, cos_ref, sin_ref, kvm_ref, q_refs, ymem_ref, slabs.at[i, 0])
    if with_kv:
        for i in chains:
            kv, lo = kvs[i], i * SUB_TILE
            k = _rope(kv[:, :MAIN_W], cos_ref[rows[i], :], sin_ref[rows[i], :])
            for g in range(N_GROUPS):
                _scatter_by_residue(k[:, g * GROUP_W:(g + 1) * GROUP_W], slabs.at[i, 1], k_refs[g], g, lo)
                _scatter_by_residue(kv[:, MAIN_W + g * GROUP_W:MAIN_W + (g + 1) * GROUP_W], slabs.at[i, 2],
                                    v_refs[g], g, lo)


def _mix_b_in(x, g0, w_in, cos_t, sin_t, kvm, l, kv=None):
    B, S, _ = x.shape
    T = MIX_TILE
    tab = pl.BlockSpec((None, T, LANES), lambda b, j: (b, j, 0))
    vec = pl.BlockSpec((1, D_MODEL), lambda b, j: (0, 0))
    grp_specs = [_dil_spec(g, T) for g in range(N_GROUPS)]
    grp_shapes = [jax.ShapeDtypeStruct(_dil_shape(B, S, g), BF16) for g in range(N_GROUPS)]
    in_specs = [
        pl.BlockSpec((None, T, D_MODEL), lambda b, j: (b, j, 0)), vec,
        pl.BlockSpec((None, D_MODEL, D_MODEL), lambda b, j: (l, 0, 0)),
        tab, tab,
        _kvm_spec(l),
    ]
    args = [x, g0, w_in, cos_t, sin_t, kvm]
    out_specs = grp_specs + [pl.BlockSpec((None, T, MEM_W), lambda b, j: (b, j, 0))]
    out_shape = grp_shapes + [jax.ShapeDtypeStruct((B, S, MEM_W), BF16)]
    if kv is not None:
        in_specs += [vec, pl.BlockSpec((D_MODEL, 2 * MAIN_W), lambda b, j: (0, 0))]
        args += list(kv)
        out_specs, out_shape = out_specs + grp_specs * 2, out_shape + grp_shapes * 2
    res = pl.pallas_call(
        functools.partial(_mix_b_in_body, with_kv=kv is not None),
        grid=(B, S // T),
        in_specs=in_specs,
        out_specs=out_specs,
        out_shape=out_shape,
        scratch_shapes=[_slab_scratch(T // SUB_TILE, 3 if kv is not None else 1, SUB_TILE)],
        compiler_params=_params(2),
        name="mix_b_in",
    )(*args)
    n = N_GROUPS
    q_ymem = (res[:n], res[n])
    return (q_ymem, (res[n + 1:2 * n + 1], res[2 * n + 1:])) if kv is not None else q_ymem


def _band_attn_body(q_ref, k_ref, v_ref, o_ref, lse_ref):
    A = ATTN_BLOCK
    n_seq, L, _ = q_ref.shape
    qi = lax.broadcasted_iota(jnp.int32, (2 * A, 2 * A), 0) & (A - 1)
    kj = lax.broadcasted_iota(jnp.int32, (2 * A, 2 * A), 1)
    band = (kj >= qi) & (kj <= qi + A)
    causal = (lax.broadcasted_iota(jnp.int32, (2 * A, A), 1)
              <= (lax.broadcasted_iota(jnp.int32, (2 * A, A), 0) & (A - 1)))
    qi2 = lax.broadcasted_iota(jnp.int32, (4 * A, 2 * A), 0)
    kj2 = lax.broadcasted_iota(jnp.int32, (4 * A, 2 * A), 1)
    causal2 = ((qi2 >= 2 * A) == (kj2 >= A)) & ((kj2 & (A - 1)) <= (qi2 & (A - 1)))
    band, causal, causal2 = [jnp.where(m, 0.0, NEG).astype(F32) for m in (band, causal, causal2)]

    first = [(r, 0) for r in range(n_seq)]
    units = [(first[i:i + 2], causal2) for i in range(0, n_seq - 1, 2)]
    if n_seq % 2:
        units.append((first[-1:], causal))
    units += [([(r, n)], band) for r in range(n_seq) for n in range(1, L // A)]

    krows = lambda n: slice(0, A) if n == 0 else slice((n - 1) * A, (n + 1) * A)
    cat = lambda parts: parts[0] if len(parts) == 1 else jnp.concatenate(parts, axis=0)
    for members, mask in units:
        for p in range(GROUP_W // LANES):
            cols = slice(p * LANES, (p + 1) * LANES)
            q = cat([_split_heads(q_ref[r, n * A:(n + 1) * A, cols]) for r, n in members])
            k = cat([k_ref[r, krows(n), cols] for r, n in members])
            v = cat([v_ref[r, krows(n), cols] for r, n in members])
            s = lax.dot_general(q, k, (((1,), (1,)), ((), ())), preferred_element_type=F32) + mask
            m = jnp.max(s, axis=-1, keepdims=True)
            e = jnp.exp2(s - m).astype(BF16)
            oe = jnp.dot(e, _with_ones(v), preferred_element_type=F32)
            for i, (r, n) in enumerate(members):
                part = slice(i * 2 * A, (i + 1) * 2 * A)
                den = _merge_heads(oe[part, LANES:])
                o_ref[r, n * A:(n + 1) * A, cols] = (_merge_heads(oe[part, :LANES]) / den).astype(BF16)
                lse_ref[r, n * A:(n + 1) * A, cols] = (
                    _merge_heads(jnp.broadcast_to(m[part], (2 * A, LANES))) + jnp.log2(den))


def _band_attn(q, k, v, g):
    B, d, L, _ = q.shape
    assert DIL_PATTERNS[g][0] // d == ATTN_BLOCK and L % ATTN_BLOCK == 0
    spec = pl.BlockSpec((None, d, L, GROUP_W), lambda b: (b, 0, 0, 0))
    return pl.pallas_call(
        _band_attn_body,
        grid=(B,),
        in_specs=[spec, spec, spec],
        out_specs=[spec, spec],
        out_shape=[jax.ShapeDtypeStruct(q.shape, BF16), jax.ShapeDtypeStruct(q.shape, F32)],
        compiler_params=_params(1),
        name=f"band_attn_g{g}",
    )(q, k, v)


def _mix_b_out_rows(x, lo, g1, o_refs, l_refs, ymem_ref, w_out_ref, slabs):
    H = x.shape[0]
    outs = [_gather_by_residue(r, slabs.at[2 * g], g, lo, H) for g, r in enumerate(o_refs)]
    lses = [_gather_by_residue(r, slabs.at[2 * g + 1], g, lo, H) for g, r in enumerate(l_refs)]
    m = jnp.maximum(jnp.maximum(lses[0], lses[1]), lses[2])
    es = [jnp.exp2(l - m) for l in lses]
    inv = 1.0 / (es[0] + es[1] + es[2])
    cols = [(outs[g] * (es[g] * inv)).astype(BF16) for g in range(N_GROUPS)]
    y = jnp.concatenate(cols + [ymem_ref[lo:lo + H, :]], axis=1)
    y = jnp.dot(y, w_out_ref[...], preferred_element_type=F32)
    return x + _rms(y, g1)


def _layer_b_body(x_ref, g_ref, o0_ref, o1_ref, o2_ref, l0_ref, l1_ref, l2_ref, ymem_ref,
                  w_out_ref, w_gu_ref, w_down_ref, out_ref, slabs):
    _, g1, g2, g3 = _gain_rows(g_ref)
    chains = range(x_ref.shape[0] // B_CHAIN)
    x1 = [_mix_b_out_rows(x_ref[i * B_CHAIN:(i + 1) * B_CHAIN, :], i * B_CHAIN, g1, (o0_ref, o1_ref, o2_ref),
                          (l0_ref, l1_ref, l2_ref), ymem_ref, w_out_ref, slabs.at[i]) for i in chains]
    for i in chains:
        out_ref[i * B_CHAIN:(i + 1) * B_CHAIN, :] = _ffn_rows(x1[i], g2, g3, w_gu_ref, w_down_ref)


def _layer_b_tail(x, gains, outs, lses, ymem, w_out, w_gu, w_down, l):
    B, S, _ = x.shape
    T = B_TILE
    tok = pl.BlockSpec((None, T, D_MODEL), lambda b, j: (b, j, 0))
    grp = [_dil_spec(g, T) for g in range(N_GROUPS)]
    layer = lambda *shape: _resident((None,) + shape, lambda b, j: (l,) + (0,) * len(shape))
    return pl.pallas_call(
        _layer_b_body,
        grid=(B, S // T),
        in_specs=[
            tok, layer(4, D_MODEL),
            *grp, *grp,
            pl.BlockSpec((None, T, MEM_W), lambda b, j: (b, j, 0)),
            layer(D_MODEL, D_MODEL), layer(D_MODEL, 2 * D_FF), layer(D_FF, D_MODEL),
        ],
        out_specs=tok,
        out_shape=jax.ShapeDtypeStruct(x.shape, F32),
        scratch_shapes=[_slab_scratch(T // B_CHAIN, 2 * N_GROUPS, B_CHAIN)],
        compiler_params=_params(2),
        name="layer_b_tail",
    )(x, gains, *outs, *lses, ymem, w_out, w_gu, w_down)


def _pool_block_diag(w_pool):
    n_layers, n_grp = w_pool.shape[:2]
    out = jnp.zeros((n_layers, MAIN_W, MAIN_W), w_pool.dtype)
    for g in range(n_grp):
        sl = slice(g * POOL_GROUP, (g + 1) * POOL_GROUP)
        out = out.at[:, sl, sl].set(w_pool[:, g])
    return out


@jax.jit
def kernel(x, mem, positions, norm_gains, mem_norm, w_in, w_mem_kv, w_out, w_pool, pool_scale,
           kv_norm, w_kv, w_gate_up, w_down):
    w_in_b = w_in.astype(BF16)
    w_out_b = w_out.astype(BF16)
    w_gu_b = w_gate_up.astype(BF16)
    w_down_b = w_down.astype(BF16)
    w_pool_b = _pool_block_diag(w_pool).astype(BF16)
    pool_scale = pool_scale.reshape(N_A_LAYERS, 1, MAIN_W)

    kvm = _mem_kv(mem, mem_norm, w_mem_kv.astype(BF16))
    cos_t, sin_t = _rope_tables(positions)
    kg = vg = None
    for l in range(DEPTH):
        if l < N_A_LAYERS:
            x = _layer_a(x, norm_gains, w_in_b, w_pool_b, pool_scale, kvm, w_out_b, w_gu_b, w_down_b, l)
            continue
        g0 = norm_gains[l, 0].reshape(1, D_MODEL)
        if l == N_A_LAYERS:
            (qg, ymem), (kg, vg) = _mix_b_in(x, g0, w_in_b, cos_t, sin_t, kvm, l,
                                             kv=(kv_norm.reshape(1, D_MODEL), w_kv.astype(BF16)))
        else:
            qg, ymem = _mix_b_in(x, g0, w_in_b, cos_t, sin_t, kvm, l)
        res = [_band_attn(qg[g], kg[g], vg[g], g) for g in range(N_GROUPS)]
        x = _layer_b_tail(x, norm_gains, [r[0] for r in res], [r[1] for r in res], ymem,
                          w_out_b, w_gu_b, w_down_b, l)
    return x
```

```python
import functools
import math

import jax
import jax.numpy as jnp
from jax import lax
from jax.experimental import pallas as pl
from jax.experimental.pallas import tpu as pltpu

D_MODEL = 1024
DEPTH = 4
N_MEM = 256
HEAD_DIM = 64
N_MEM_HEADS = 4
MEM_W = N_MEM_HEADS * HEAD_DIM
MAIN_W = D_MODEL - MEM_W
POOL_WINDOWS = (2, 4, 8, 16)
POOL_GROUP = MAIN_W // len(POOL_WINDOWS)
POOL_HALO = max(POOL_WINDOWS)
DIL_PATTERNS = ((128, 1), (512, 4), (2048, 16))
N_GROUPS = len(DIL_PATTERNS)
GROUP_W = MAIN_W // N_GROUPS
N_A_LAYERS = DEPTH // 2
D_FF = ((8 * D_MODEL + 3 * 256 - 1) // (3 * 256)) * 256
ROPE_THETA = 10000.0
EPS = 1e-6
NEG = -1e30
Q_SCALE = HEAD_DIM ** -0.5 * math.log2(math.e)

LANES = 128
ATTN_BLOCK = 128
A_TILE, A_CHAIN = 1024, 512
B_TILE, B_CHAIN = 512, 256
MIX_TILE, SUB_TILE = 2048, 512
VMEM_LIMIT = 60 * 1024 * 1024

BF16 = jnp.bfloat16
F32 = jnp.float32


def _rms(xf, gain):
    return xf * lax.rsqrt(jnp.mean(xf * xf, axis=-1, keepdims=True) + EPS) * gain


def _params(n_grid_dims):
    return pltpu.CompilerParams(
        dimension_semantics=("arbitrary",) * n_grid_dims,
        vmem_limit_bytes=VMEM_LIMIT)


def _head0_lanes():
    return lax.broadcasted_iota(jnp.int32, (1, LANES), 1) < HEAD_DIM


MEM_ROWS = 4 * N_MEM


def _mem_kv_body(mem_ref, g_ref, w_ref, o_ref):
    x = mem_ref[...]
    xn = x * lax.rsqrt(jnp.mean(x * x, axis=-1, keepdims=True) + EPS)
    for l in range(DEPTH):
        h = (xn * g_ref[l]).astype(BF16)
        o_ref[l] = jnp.dot(h, w_ref[l], preferred_element_type=F32).astype(BF16)


def _mem_kv(mem, mem_norm, w_mem_kv):
    rows = mem.shape[0] * N_MEM
    return pl.pallas_call(
        _mem_kv_body,
        grid=(rows // MEM_ROWS,),
        in_specs=[
            pl.BlockSpec((MEM_ROWS, D_MODEL), lambda i: (i, 0)),
            pl.BlockSpec((DEPTH, 1, D_MODEL), lambda i: (0, 0, 0)),
            pl.BlockSpec((DEPTH, D_MODEL, 2 * MEM_W), lambda i: (0, 0, 0)),
        ],
        out_specs=pl.BlockSpec((DEPTH, MEM_ROWS, 2 * MEM_W), lambda i: (0, i, 0)),
        out_shape=jax.ShapeDtypeStruct((DEPTH, rows, 2 * MEM_W), BF16),
        compiler_params=_params(1),
        name="mem_kv",
    )(mem.reshape(rows, D_MODEL), mem_norm.reshape(DEPTH, 1, D_MODEL), w_mem_kv)


def _kvm_spec(l):
    return pl.BlockSpec((None, N_MEM, 2 * MEM_W), lambda b, j: (l, b, 0))


def _split_heads(q):
    head0 = _head0_lanes()
    zero = jnp.zeros_like(q)
    return jnp.concatenate([jnp.where(head0, q, zero), jnp.where(head0, zero, q)], axis=0)


def _merge_heads(a):
    t = a.shape[0] // 2
    return jnp.where(_head0_lanes(), a[:t], a[t:])


def _with_ones(v):
    return jnp.concatenate([v, jnp.ones_like(v)], axis=1)


def _memory_attention(zm, kvm):
    q = (zm * Q_SCALE).astype(BF16)
    pairs = []
    for p in range(MEM_W // LANES):
        kp = kvm[:, p * LANES:(p + 1) * LANES]
        vp = kvm[:, MEM_W + p * LANES:MEM_W + (p + 1) * LANES]
        s = lax.dot_general(_split_heads(q[:, p * LANES:(p + 1) * LANES]), kp,
                            (((1,), (1,)), ((), ())), preferred_element_type=F32)
        e = jnp.exp2(s - jnp.max(s, axis=-1, keepdims=True)).astype(BF16)
        oe = jnp.dot(e, _with_ones(vp), preferred_element_type=F32)
        pairs.append(_merge_heads(oe[:, :LANES]) / _merge_heads(oe[:, LANES:]))
    return jnp.concatenate(pairs, axis=1)


def _rope_tables_body(pos_ref, freq_ref, cos_ref, sin_ref):
    ang = freq_ref[...] * pos_ref[...].astype(F32)
    c = jnp.cos(ang)
    s = jnp.sin(ang)
    cos_ref[...] = jnp.concatenate([c, c, c, c], axis=0).T
    sin_ref[...] = jnp.concatenate([-s, s, -s, s], axis=0).T


def _rope_tables(positions):
    B, S = positions.shape
    half = HEAD_DIM // 2
    freqs = ROPE_THETA ** (-jnp.arange(half, dtype=F32) / half)
    tab = pl.BlockSpec((None, S, LANES), lambda b: (b, 0, 0))
    return pl.pallas_call(
        _rope_tables_body,
        grid=(B,),
        in_specs=[pl.BlockSpec((None, 1, S), lambda b: (b, 0, 0)),
                  pl.BlockSpec((half, 1), lambda b: (0, 0))],
        out_specs=[tab, tab],
        out_shape=[jax.ShapeDtypeStruct((B, S, LANES), F32)] * 2,
        compiler_params=_params(1),
        name="rope_tables",
    )(positions.reshape(B, 1, S), freqs.reshape(half, 1))


def _rope(t, cos, sin_signed):
    first_half = (lax.broadcasted_iota(jnp.int32, (1, LANES), 1) % HEAD_DIM) < (HEAD_DIM // 2)
    cols = []
    for c in range(t.shape[1] // LANES):
        tc = t[:, c * LANES:(c + 1) * LANES]
        partner = jnp.where(first_half,
                            pltpu.roll(tc, LANES - HEAD_DIM // 2, 1),
                            pltpu.roll(tc, HEAD_DIM // 2, 1))
        cols.append(tc * cos + partner * sin_signed)
    return jnp.concatenate(cols, axis=1)


def _pool_means(zh):
    lane = lax.broadcasted_iota(jnp.int32, (1, LANES), 1)
    low = lane < (POOL_GROUP - LANES)
    s2 = zh + pltpu.roll(zh, 1, 0)
    a = s2[:, LANES:]
    s4 = a + pltpu.roll(a, 2, 0)
    a = s4[:, 2 * LANES:]
    s8 = a + pltpu.roll(a, 4, 0)
    a = s8[:, LANES:]
    s16 = a + pltpu.roll(a, 8, 0)
    h = POOL_HALO
    blk = lambda s, c: s[h:, c * LANES:(c + 1) * LANES]
    cols = [
        blk(s2, 0) * 0.5,
        jnp.where(low, blk(s2, 1) * 0.5, blk(s4, 0) * 0.25),
        blk(s4, 1) * 0.25,
        blk(s8, 0) * 0.125,
        jnp.where(low, blk(s8, 1) * 0.125, blk(s16, 0) * 0.0625),
        blk(s16, 1) * 0.0625,
    ]
    return jnp.concatenate(cols, axis=1)


def _mix_a_rows(x, z, lo, t0, g1, w_pool_ref, scale_ref, kvm_ref, w_out_ref, zbuf):
    n_rows = x.shape[0]
    u = z[:, :MAIN_W]
    zbuf[POOL_HALO + lo:POOL_HALO + lo + n_rows, :] = u
    mean = _pool_means(zbuf[lo:lo + n_rows + POOL_HALO, :])
    t = lax.broadcasted_iota(jnp.int32, (POOL_HALO, MAIN_W), 0) + (t0 + lo)
    col = lax.broadcasted_iota(jnp.int32, (POOL_HALO, MAIN_W), 1)
    win = jnp.full((POOL_HALO, MAIN_W), POOL_WINDOWS[0], jnp.int32)
    for gi in range(1, len(POOL_WINDOWS)):
        win = jnp.where(col >= gi * POOL_GROUP, POOL_WINDOWS[gi], win)
    fix = win.astype(F32) / jnp.minimum(t + 1, win).astype(F32)
    mean = jnp.concatenate([mean[:POOL_HALO] * fix, mean[POOL_HALO:]], axis=0)

    p = (mean - u).astype(BF16)
    y_main = jnp.dot(p, w_pool_ref[...], preferred_element_type=F32) * scale_ref[...]
    y_mem = _memory_attention(z[:, MAIN_W:], kvm_ref[...])
    y = jnp.concatenate([y_main, y_mem], axis=1).astype(BF16)
    y = jnp.dot(y, w_out_ref[...], preferred_element_type=F32)
    return x + _rms(y, g1)


def _ffn_rows(x, g2, g3, w_gu_ref, w_down_ref):
    h = _rms(x, g2).astype(BF16)
    gate = jnp.dot(h, w_gu_ref[:, :D_FF], preferred_element_type=F32)
    up = jnp.dot(h, w_gu_ref[:, D_FF:], preferred_element_type=F32)
    act = (gate * jax.nn.sigmoid(gate) * up).astype(BF16)
    y = jnp.dot(act, w_down_ref[...], preferred_element_type=F32)
    return x + _rms(y, g3)


def _resident(block_shape, index_map):
    return pl.BlockSpec(block_shape, index_map, pipeline_mode=pl.Buffered(1))


def _gain_rows(g_ref):
    return [g_ref[i:i + 1, :] for i in range(g_ref.shape[0])]


def _layer_a_body(x_ref, g_ref, w_in_ref, w_pool_ref, scale_ref, kvm_ref, w_out_ref, w_gu_ref, w_down_ref,
                  o_ref, zbuf):
    T = x_ref.shape[0]
    g0, g1, g2, g3 = _gain_rows(g_ref)

    @pl.when(pl.program_id(1) == 0)
    def _():
        zbuf[0:POOL_HALO, :] = jnp.zeros((POOL_HALO, MAIN_W), F32)

    chains = range(T // A_CHAIN)
    x1 = []
    for i in chains:
        x = x_ref[i * A_CHAIN:(i + 1) * A_CHAIN, :]
        z = jnp.dot(_rms(x, g0).astype(BF16), w_in_ref[...], preferred_element_type=F32)
        x1.append(_mix_a_rows(x, z, i * A_CHAIN, pl.program_id(1) * T, g1,
                              w_pool_ref, scale_ref, kvm_ref, w_out_ref, zbuf))
    zbuf[0:POOL_HALO, :] = zbuf[T:T + POOL_HALO, :]
    for i in chains:
        o_ref[i * A_CHAIN:(i + 1) * A_CHAIN, :] = _ffn_rows(x1[i], g2, g3, w_gu_ref, w_down_ref)


def _layer_a(x, gains, w_in, w_pool_bd, pool_scale, kvm, w_out, w_gu, w_down, l):
    B, S, _ = x.shape
    T = A_TILE
    tok = pl.BlockSpec((None, T, D_MODEL), lambda b, j: (b, j, 0))
    layer = lambda *shape: _resident((None,) + shape, lambda b, j: (l,) + (0,) * len(shape))
    return pl.pallas_call(
        _layer_a_body,
        grid=(B, S // T),
        in_specs=[
            tok, layer(4, D_MODEL),
            layer(D_MODEL, D_MODEL), layer(MAIN_W, MAIN_W), layer(1, MAIN_W),
            _kvm_spec(l),
            layer(D_MODEL, D_MODEL), layer(D_MODEL, 2 * D_FF), layer(D_FF, D_MODEL),
        ],
        out_specs=tok,
        out_shape=jax.ShapeDtypeStruct(x.shape, F32),
        scratch_shapes=[pltpu.VMEM((T + POOL_HALO, MAIN_W), F32)],
        compiler_params=_params(2),
        name="layer_a",
    )(x, gains, w_in, w_pool_bd, pool_scale, kvm, w_out, w_gu, w_down)


def _dil_shape(B, S, g):
    d = DIL_PATTERNS[g][1]
    return (B, d, S // d, GROUP_W)


def _dil_spec(g, T):
    d = DIL_PATTERNS[g][1]
    return pl.BlockSpec((None, d, T // d, GROUP_W), lambda b, j: (b, 0, j, 0))


def _scatter_by_residue(val, slabs, out_ref, g, lo):
    d = DIL_PATTERNS[g][1]
    H = val.shape[0]
    if d == 1:
        out_ref[0, lo:lo + H, :] = val.astype(BF16)
        return
    for p in range(GROUP_W // LANES):
        slabs[p] = val[:, p * LANES:(p + 1) * LANES]
    for r in range(d):
        for p in range(GROUP_W // LANES):
            out_ref[r, lo // d:(lo + H) // d, p * LANES:(p + 1) * LANES] = (
                slabs[p, pl.ds(r, H // d, stride=d), :].astype(BF16))


def _gather_by_residue(in_ref, slabs, g, lo, H):
    d = DIL_PATTERNS[g][1]
    if d == 1:
        return in_ref[0, lo:lo + H, :].astype(F32)
    for r in range(d):
        for p in range(GROUP_W // LANES):
            slabs[p, pl.ds(r, H // d, stride=d), :] = (
                in_ref[r, lo // d:(lo + H) // d, p * LANES:(p + 1) * LANES].astype(F32))
    return jnp.concatenate([slabs[p] for p in range(GROUP_W // LANES)], axis=1)


def _slab_scratch(n_chains, n, chain_rows):
    return pltpu.VMEM((n_chains, n, GROUP_W // LANES, chain_rows, LANES), F32)


def _mix_b_in_rows(z, lo, cos_ref, sin_ref, kvm_ref, q_refs, ymem_ref, slabs):
    rows = slice(lo, lo + z.shape[0])
    ymem_ref[rows, :] = _memory_attention(z[:, MAIN_W:], kvm_ref[...]).astype(BF16)
    q = _rope(z[:, :MAIN_W], cos_ref[rows, :], sin_ref[rows, :]) * Q_SCALE
    for g, q_ref in enumerate(q_refs):
        _scatter_by_residue(q[:, g * GROUP_W:(g + 1) * GROUP_W], slabs, q_ref, g, lo)


def _mix_b_in_body(*refs, with_kv):
    x_ref, g0_ref, w_in_ref, cos_ref, sin_ref, kvm_ref = refs[:6]
    n_in = 8 if with_kv else 6
    q_refs, ymem_ref, slabs = refs[n_in:n_in + 3], refs[n_in + 3], refs[-1]
    chains = range(x_ref.shape[0] // SUB_TILE)
    rows = [slice(i * SUB_TILE, (i + 1) * SUB_TILE) for i in chains]
    xn = []
    for i in chains:
        x = x_ref[rows[i], :]
        xn.append(x * lax.rsqrt(jnp.mean(x * x, axis=-1, keepdims=True) + EPS))
    z = [jnp.dot((xn[i] * g0_ref[...]).astype(BF16), w_in_ref[...], preferred_element_type=F32) for i in chains]
    if with_kv:
        kv_norm_ref, w_kv_ref = refs[6:8]
        k_refs, v_refs = refs[n_in + 4:n_in + 7], refs[n_in + 7:n_in + 10]
        kvs = [jnp.dot((xn[i] * kv_norm_ref[...]).astype(BF16), w_kv_ref[...], preferred_element_type=F32)
               for i in chains]
    for i in chains:
        _mix_b_in_rows(z[i], i * SUB_TILE, cos_ref, sin_ref, kvm_ref, q_refs, ymem_ref, slabs.at[i, 0])
    if with_kv:
        for i in chains:
            kv, lo = kvs[i], i * SUB_TILE
            k = _rope(kv[:, :MAIN_W], cos_ref[rows[i], :], sin_ref[rows[i], :])
            for g in range(N_GROUPS):
                _scatter_by_residue(k[:, g * GROUP_W:(g + 1) * GROUP_W], slabs.at[i, 1], k_refs[g], g, lo)
                _scatter_by_residue(kv[:, MAIN_W + g * GROUP_W:MAIN_W + (g + 1) * GROUP_W], slabs.at[i, 2],
                                    v_refs[g], g, lo)


def _mix_b_in(x, g0, w_in, cos_t, sin_t, kvm, l, kv=None):
    B, S, _ = x.shape
    T = MIX_TILE
    tab = pl.BlockSpec((None, T, LANES), lambda b, j: (b, j, 0))
    vec = pl.BlockSpec((1, D_MODEL), lambda b, j: (0, 0))
    grp_specs = [_dil_spec(g, T) for g in range(N_GROUPS)]
    grp_shapes = [jax.ShapeDtypeStruct(_dil_shape(B, S, g), BF16) for g in range(N_GROUPS)]
    in_specs = [
        pl.BlockSpec((None, T, D_MODEL), lambda b, j: (b, j, 0)), vec,
        pl.BlockSpec((None, D_MODEL, D_MODEL), lambda b, j: (l, 0, 0)),
        tab, tab,
        _kvm_spec(l),
    ]
    args = [x, g0, w_in, cos_t, sin_t, kvm]
    out_specs = grp_specs + [pl.BlockSpec((None, T, MEM_W), lambda b, j: (b, j, 0))]
    out_shape = grp_shapes + [jax.ShapeDtypeStruct((B, S, MEM_W), BF16)]
    if kv is not None:
        in_specs += [vec, pl.BlockSpec((D_MODEL, 2 * MAIN_W), lambda b, j: (0, 0))]
        args += list(kv)
        out_specs, out_shape = out_specs + grp_specs * 2, out_shape + grp_shapes * 2
    res = pl.pallas_call(
        functools.partial(_mix_b_in_body, with_kv=kv is not None),
        grid=(B, S // T),
        in_specs=in_specs,
        out_specs=out_specs,
        out_shape=out_shape,
        scratch_shapes=[_slab_scratch(T // SUB_TILE, 3 if kv is not None else 1, SUB_TILE)],
        compiler_params=_params(2),
        name="mix_b_in",
    )(*args)
    n = N_GROUPS
    q_ymem = (res[:n], res[n])
    return (q_ymem, (res[n + 1:2 * n + 1], res[2 * n + 1:])) if kv is not None else q_ymem


def _band_attn_body(q_ref, k_ref, v_ref, o_ref, lse_ref):
    A = ATTN_BLOCK
    n_seq, L, _ = q_ref.shape
    qi = lax.broadcasted_iota(jnp.int32, (2 * A, 2 * A), 0) & (A - 1)
    kj = lax.broadcasted_iota(jnp.int32, (2 * A, 2 * A), 1)
    band = (kj >= qi) & (kj <= qi + A)
    causal = (lax.broadcasted_iota(jnp.int32, (2 * A, A), 1)
              <= (lax.broadcasted_iota(jnp.int32, (2 * A, A), 0) & (A - 1)))
    qi2 = lax.broadcasted_iota(jnp.int32, (4 * A, 2 * A), 0)
    kj2 = lax.broadcasted_iota(jnp.int32, (4 * A, 2 * A), 1)
    causal2 = ((qi2 >= 2 * A) == (kj2 >= A)) & ((kj2 & (A - 1)) <= (qi2 & (A - 1)))
    band, causal, causal2 = [jnp.where(m, 0.0, NEG).astype(F32) for m in (band, causal, causal2)]

    first = [(r, 0) for r in range(n_seq)]
    units = [(first[i:i + 2], causal2) for i in range(0, n_seq - 1, 2)]
    if n_seq % 2:
        units.append((first[-1:], causal))
    units += [([(r, n)], band) for r in range(n_seq) for n in range(1, L // A)]

    krows = lambda n: slice(0, A) if n == 0 else slice((n - 1) * A, (n + 1) * A)
    cat = lambda parts: parts[0] if len(parts) == 1 else jnp.concatenate(parts, axis=0)
    for members, mask in units:
        for p in range(GROUP_W // LANES):
            cols = slice(p * LANES, (p + 1) * LANES)
            q = cat([_split_heads(q_ref[r, n * A:(n + 1) * A, cols]) for r, n in members])
            k = cat([k_ref[r, krows(n), cols] for r, n in members])
            v = cat([v_ref[r, krows(n), cols] for r, n in members])
            s = lax.dot_general(q, k, (((1,), (1,)), ((), ())), preferred_element_type=F32) + mask
            m = jnp.max(s, axis=-1, keepdims=True)
            e = jnp.exp2(s - m).astype(BF16)
            oe = jnp.dot(e, _with_ones(v), preferred_element_type=F32)
            for i, (r, n) in enumerate(members):
                part = slice(i * 2 * A, (i + 1) * 2 * A)
                den = _merge_heads(oe[part, LANES:])
                o_ref[r, n * A:(n + 1) * A, cols] = (_merge_heads(oe[part, :LANES]) / den).astype(BF16)
                lse_ref[r, n * A:(n + 1) * A, cols] = (
                    _merge_heads(jnp.broadcast_to(m[part], (2 * A, LANES))) + jnp.log2(den))


def _band_attn(q, k, v, g):
    B, d, L, _ = q.shape
    assert DIL_PATTERNS[g][0] // d == ATTN_BLOCK and L % ATTN_BLOCK == 0
    spec = pl.BlockSpec((None, d, L, GROUP_W), lambda b: (b, 0, 0, 0))
    return pl.pallas_call(
        _band_attn_body,
        grid=(B,),
        in_specs=[spec, spec, spec],
        out_specs=[spec, spec],
        out_shape=[jax.ShapeDtypeStruct(q.shape, BF16), jax.ShapeDtypeStruct(q.shape, F32)],
        compiler_params=_params(1),
        name=f"band_attn_g{g}",
    )(q, k, v)


def _mix_b_out_rows(x, lo, g1, o_refs, l_refs, ymem_ref, w_out_ref, slabs):
    H = x.shape[0]
    outs = [_gather_by_residue(r, slabs.at[2 * g], g, lo, H) for g, r in enumerate(o_refs)]
    lses = [_gather_by_residue(r, slabs.at[2 * g + 1], g, lo, H) for g, r in enumerate(l_refs)]
    m = jnp.maximum(jnp.maximum(lses[0], lses[1]), lses[2])
    es = [jnp.exp2(l - m) for l in lses]
    inv = 1.0 / (es[0] + es[1] + es[2])
    cols = [(outs[g] * (es[g] * inv)).astype(BF16) for g in range(N_GROUPS)]
    y = jnp.concatenate(cols + [ymem_ref[lo:lo + H, :]], axis=1)
    y = jnp.dot(y, w_out_ref[...], preferred_element_type=F32)
    return x + _rms(y, g1)


def _layer_b_body(x_ref, g_ref, o0_ref, o1_ref, o2_ref, l0_ref, l1_ref, l2_ref, ymem_ref,
                  w_out_ref, w_gu_ref, w_down_ref, out_ref, slabs):
    _, g1, g2, g3 = _gain_rows(g_ref)
    chains = range(x_ref.shape[0] // B_CHAIN)
    x1 = [_mix_b_out_rows(x_ref[i * B_CHAIN:(i + 1) * B_CHAIN, :], i * B_CHAIN, g1, (o0_ref, o1_ref, o2_ref),
                          (l0_ref, l1_ref, l2_ref), ymem_ref, w_out_ref, slabs.at[i]) for i in chains]
    for i in chains:
        out_ref[i * B_CHAIN:(i + 1) * B_CHAIN, :] = _ffn_rows(x1[i], g2, g3, w_gu_ref, w_down_ref)


def _layer_b_tail(x, gains, outs, lses, ymem, w_out, w_gu, w_down, l):
    B, S, _ = x.shape
    T = B_TILE
    tok = pl.BlockSpec((None, T, D_MODEL), lambda b, j: (b, j, 0))
    grp = [_dil_spec(g, T) for g in range(N_GROUPS)]
    layer = lambda *shape: _resident((None,) + shape, lambda b, j: (l,) + (0,) * len(shape))
    return pl.pallas_call(
        _layer_b_body,
        grid=(B, S // T),
        in_specs=[
            tok, layer(4, D_MODEL),
            *grp, *grp,
            pl.BlockSpec((None, T, MEM_W), lambda b, j: (b, j, 0)),
            layer(D_MODEL, D_MODEL), layer(D_MODEL, 2 * D_FF), layer(D_FF, D_MODEL),
        ],
        out_specs=tok,
        out_shape=jax.ShapeDtypeStruct(x.shape, F32),
        scratch_shapes=[_slab_scratch(T // B_CHAIN, 2 * N_GROUPS, B_CHAIN)],
        compiler_params=_params(2),
        name="layer_b_tail",
    )(x, gains, *outs, *lses, ymem, w_out, w_gu, w_down)


def _pool_block_diag(w_pool):
    n_layers, n_grp = w_pool.shape[:2]
    out = jnp.zeros((n_layers, MAIN_W, MAIN_W), w_pool.dtype)
    for g in range(n_grp):
        sl = slice(g * POOL_GROUP, (g + 1) * POOL_GROUP)
        out = out.at[:, sl, sl].set(w_pool[:, g])
    return out


@jax.jit
def kernel(x, mem, positions, norm_gains, mem_norm, w_in, w_mem_kv, w_out, w_pool, pool_scale,
           kv_norm, w_kv, w_gate_up, w_down):
    w_in_b = w_in.astype(BF16)
    w_out_b = w_out.astype(BF16)
    w_gu_b = w_gate_up.astype(BF16)
    w_down_b = w_down.astype(BF16)
    w_pool_b = _pool_block_diag(w_pool).astype(BF16)
    pool_scale = pool_scale.reshape(N_A_LAYERS, 1, MAIN_W)

    kvm = _mem_kv(mem, mem_norm, w_mem_kv.astype(BF16))
    cos_t, sin_t = _rope_tables(positions)
    kg = vg = None
    for l in range(DEPTH):
        if l < N_A_LAYERS:
            x = _layer_a(x, norm_gains, w_in_b, w_pool_b, pool_scale, kvm, w_out_b, w_gu_b, w_down_b, l)
            continue
        g0 = norm_gains[l, 0].reshape(1, D_MODEL)
        if l == N_A_LAYERS:
            (qg, ymem), (kg, vg) = _mix_b_in(x, g0, w_in_b, cos_t, sin_t, kvm, l,
                                             kv=(kv_norm.reshape(1, D_MODEL), w_kv.astype(BF16)))
        else:
            qg, ymem = _mix_b_in(x, g0, w_in_b, cos_t, sin_t, kvm, l)
        res = [_band_attn(qg[g], kg[g], vg[g], g) for g in range(N_GROUPS)]
        x = _layer_b_tail(x, norm_gains, [r[0] for r in res], [r[1] for r in res], ymem,
                          w_out_b, w_gu_b, w_down_b, l)
    return x
```

```python
import functools
import math

import jax
import jax.numpy as jnp
from jax import lax
from jax.experimental import pallas as pl
from jax.experimental.pallas import tpu as pltpu

D_MODEL = 1024
DEPTH = 4
N_MEM = 256
HEAD_DIM = 64
N_MEM_HEADS = 4
MEM_W = N_MEM_HEADS * HEAD_DIM
MAIN_W = D_MODEL - MEM_W
POOL_WINDOWS = (2, 4, 8, 16)
POOL_GROUP = MAIN_W // len(POOL_WINDOWS)
POOL_HALO = max(POOL_WINDOWS)
DIL_PATTERNS = ((128, 1), (512, 4), (2048, 16))
N_GROUPS = len(DIL_PATTERNS)
GROUP_W = MAIN_W // N_GROUPS
N_A_LAYERS = DEPTH // 2
D_FF = ((8 * D_MODEL + 3 * 256 - 1) // (3 * 256)) * 256
ROPE_THETA = 10000.0
EPS = 1e-6
NEG = -1e30
Q_SCALE = HEAD_DIM ** -0.5 * math.log2(math.e)

LANES = 128
ATTN_BLOCK = 128
A_TILE, A_CHAIN = 1024, 512
B_TILE, B_CHAIN = 512, 256
MIX_TILE, SUB_TILE = 2048, 512
VMEM_LIMIT = 60 * 1024 * 1024

BF16 = jnp.bfloat16
F32 = jnp.float32


def _rms(xf, gain):
    return xf * lax.rsqrt(jnp.mean(xf * xf, axis=-1, keepdims=True) + EPS) * gain


def _params(n_grid_dims):
    return pltpu.CompilerParams(
        dimension_semantics=("arbitrary",) * n_grid_dims,
        vmem_limit_bytes=VMEM_LIMIT)


def _head0_lanes():
    return lax.broadcasted_iota(jnp.int32, (1, LANES), 1) < HEAD_DIM


MEM_ROWS = 4 * N_MEM


def _mem_kv_body(mem_ref, g_ref, w_ref, o_ref):
    x = mem_ref[...]
    xn = x * lax.rsqrt(jnp.mean(x * x, axis=-1, keepdims=True) + EPS)
    for l in range(DEPTH):
        h = (xn * g_ref[l]).astype(BF16)
        o_ref[l] = jnp.dot(h, w_ref[l], preferred_element_type=F32).astype(BF16)


def _mem_kv(mem, mem_norm, w_mem_kv):
    rows = mem.shape[0] * N_MEM
    return pl.pallas_call(
        _mem_kv_body,
        grid=(rows // MEM_ROWS,),
        in_specs=[
            pl.BlockSpec((MEM_ROWS, D_MODEL), lambda i: (i, 0)),
            pl.BlockSpec((DEPTH, 1, D_MODEL), lambda i: (0, 0, 0)),
            pl.BlockSpec((DEPTH, D_MODEL, 2 * MEM_W), lambda i: (0, 0, 0)),
        ],
        out_specs=pl.BlockSpec((DEPTH, MEM_ROWS, 2 * MEM_W), lambda i: (0, i, 0)),
        out_shape=jax.ShapeDtypeStruct((DEPTH, rows, 2 * MEM_W), BF16),
        compiler_params=_params(1),
        name="mem_kv",
    )(mem.reshape(rows, D_MODEL), mem_norm.reshape(DEPTH, 1, D_MODEL), w_mem_kv)


def _kvm_spec(l):
    return pl.BlockSpec((None, N_MEM, 2 * MEM_W), lambda b, j: (l, b, 0))


def _split_heads(q):
    head0 = _head0_lanes()
    zero = jnp.zeros_like(q)
    return jnp.concatenate([jnp.where(head0, q, zero), jnp.where(head0, zero, q)], axis=0)


def _merge_heads(a):
    t = a.shape[0] // 2
    return jnp.where(_head0_lanes(), a[:t], a[t:])


def _with_ones(v):
    return jnp.concatenate([v, jnp.ones_like(v)], axis=1)


def _memory_attention(zm, kvm):
    q = (zm * Q_SCALE).astype(BF16)
    pairs = []
    for p in range(MEM_W // LANES):
        kp = kvm[:, p * LANES:(p + 1) * LANES]
        vp = kvm[:, MEM_W + p * LANES:MEM_W + (p + 1) * LANES]
        s = lax.dot_general(_split_heads(q[:, p * LANES:(p + 1) * LANES]), kp,
                            (((1,), (1,)), ((), ())), preferred_element_type=F32)
        e = jnp.exp2(s - jnp.max(s, axis=-1, keepdims=True)).astype(BF16)
        oe = jnp.dot(e, _with_ones(vp), preferred_element_type=F32)
        pairs.append(_merge_heads(oe[:, :LANES]) / _merge_heads(oe[:, LANES:]))
    return jnp.concatenate(pairs, axis=1)


def _rope_tables_body(pos_ref, freq_ref, cos_ref, sin_ref):
    ang = freq_ref[...] * pos_ref[...].astype(F32)
    c = jnp.cos(ang)
    s = jnp.sin(ang)
    cos_ref[...] = jnp.concatenate([c, c, c, c], axis=0).T
    sin_ref[...] = jnp.concatenate([-s, s, -s, s], axis=0).T


def _rope_tables(positions):
    B, S = positions.shape
    half = HEAD_DIM // 2
    freqs = ROPE_THETA ** (-jnp.arange(half, dtype=F32) / half)
    tab = pl.BlockSpec((None, S, LANES), lambda b: (b, 0, 0))
    return pl.pallas_call(
        _rope_tables_body,
        grid=(B,),
        in_specs=[pl.BlockSpec((None, 1, S), lambda b: (b, 0, 0)),
                  pl.BlockSpec((half, 1), lambda b: (0, 0))],
        out_specs=[tab, tab],
        out_shape=[jax.ShapeDtypeStruct((B, S, LANES), F32)] * 2,
        compiler_params=_params(1),
        name="rope_tables",
    )(positions.reshape(B, 1, S), freqs.reshape(half, 1))


def _rope(t, cos, sin_signed):
    first_half = (lax.broadcasted_iota(jnp.int32, (1, LANES), 1) % HEAD_DIM) < (HEAD_DIM // 2)
    cols = []
    for c in range(t.shape[1] // LANES):
        tc = t[:, c * LANES:(c + 1) * LANES]
        partner = jnp.where(first_half,
                            pltpu.roll(tc, LANES - HEAD_DIM // 2, 1),
                            pltpu.roll(tc, HEAD_DIM // 2, 1))
        cols.append(tc * cos + partner * sin_signed)
    return jnp.concatenate(cols, axis=1)


def _pool_means(zh):
    lane = lax.broadcasted_iota(jnp.int32, (1, LANES), 1)
    low = lane < (POOL_GROUP - LANES)
    s2 = zh + pltpu.roll(zh, 1, 0)
    a = s2[:, LANES:]
    s4 = a + pltpu.roll(a, 2, 0)
    a = s4[:, 2 * LANES:]
    s8 = a + pltpu.roll(a, 4, 0)
    a = s8[:, LANES:]
    s16 = a + pltpu.roll(a, 8, 0)
    h = POOL_HALO
    blk = lambda s, c: s[h:, c * LANES:(c + 1) * LANES]
    cols = [
        blk(s2, 0) * 0.5,
        jnp.where(low, blk(s2, 1) * 0.5, blk(s4, 0) * 0.25),
        blk(s4, 1) * 0.25,
        blk(s8, 0) * 0.125,
        jnp.where(low, blk(s8, 1) * 0.125, blk(s16, 0) * 0.0625),
        blk(s16, 1) * 0.0625,
    ]
    return jnp.concatenate(cols, axis=1)


def _mix_a_rows(x, z, lo, t0, g1, w_pool_ref, scale_ref, kvm_ref, w_out_ref, zbuf):
    n_rows = x.shape[0]
    u = z[:, :MAIN_W]
    zbuf[POOL_HALO + lo:POOL_HALO + lo + n_rows, :] = u
    mean = _pool_means(zbuf[lo:lo + n_rows + POOL_HALO, :])
    t = lax.broadcasted_iota(jnp.int32, (POOL_HALO, MAIN_W), 0) + (t0 + lo)
    col = lax.broadcasted_iota(jnp.int32, (POOL_HALO, MAIN_W), 1)
    win = jnp.full((POOL_HALO, MAIN_W), POOL_WINDOWS[0], jnp.int32)
    for gi in range(1, len(POOL_WINDOWS)):
        win = jnp.where(col >= gi * POOL_GROUP, POOL_WINDOWS[gi], win)
    fix = win.astype(F32) / jnp.minimum(t + 1, win).astype(F32)
    mean = jnp.concatenate([mean[:POOL_HALO] * fix, mean[POOL_HALO:]], axis=0)

    p = (mean - u).astype(BF16)
    y_main = jnp.dot(p, w_pool_ref[...], preferred_element_type=F32) * scale_ref[...]
    y_mem = _memory_attention(z[:, MAIN_W:], kvm_ref[...])
    y = jnp.concatenate([y_main, y_mem], axis=1).astype(BF16)
    y = jnp.dot(y, w_out_ref[...], preferred_element_type=F32)
    return x + _rms(y, g1)


def _ffn_rows(x, g2, g3, w_gu_ref, w_down_ref):
    h = _rms(x, g2).astype(BF16)
    gate = jnp.dot(h, w_gu_ref[:, :D_FF], preferred_element_type=F32)
    up = jnp.dot(h, w_gu_ref[:, D_FF:], preferred_element_type=F32)
    act = (gate * jax.nn.sigmoid(gate) * up).astype(BF16)
    y = jnp.dot(act, w_down_ref[...], preferred_element_type=F32)
    return x + _rms(y, g3)


def _resident(block_shape, index_map):
    return pl.BlockSpec(block_shape, index_map, pipeline_mode=pl.Buffered(1))


def _gain_rows(g_ref):
    return [g_ref[i:i + 1, :] for i in range(g_ref.shape[0])]


def _layer_a_body(x_ref, g_ref, w_in_ref, w_pool_ref, scale_ref, kvm_ref, w_out_ref, w_gu_ref, w_down_ref,
                  o_ref, zbuf):
    T = x_ref.shape[0]
    g0, g1, g2, g3 = _gain_rows(g_ref)

    @pl.when(pl.program_id(1) == 0)
    def _():
        zbuf[0:POOL_HALO, :] = jnp.zeros((POOL_HALO, MAIN_W), F32)

    chains = range(T // A_CHAIN)
    x1 = []
    for i in chains:
        x = x_ref[i * A_CHAIN:(i + 1) * A_CHAIN, :]
        r = lax.rsqrt(jnp.mean(x * x, axis=-1, keepdims=True) + EPS)
        z = jnp.dot((x * g0).astype(BF16), w_in_ref[...], preferred_element_type=F32) * r
        x1.append(_mix_a_rows(x, z, i * A_CHAIN, pl.program_id(1) * T, g1,
                              w_pool_ref, scale_ref, kvm_ref, w_out_ref, zbuf))
    zbuf[0:POOL_HALO, :] = zbuf[T:T + POOL_HALO, :]
    for i in chains:
        o_ref[i * A_CHAIN:(i + 1) * A_CHAIN, :] = _ffn_rows(x1[i], g2, g3, w_gu_ref, w_down_ref)


def _layer_a(x, gains, w_in, w_pool_bd, pool_scale, kvm, w_out, w_gu, w_down, l):
    B, S, _ = x.shape
    T = A_TILE
    tok = pl.BlockSpec((None, T, D_MODEL), lambda b, j: (b, j, 0))
    layer = lambda *shape: _resident((None,) + shape, lambda b, j: (l,) + (0,) * len(shape))
    return pl.pallas_call(
        _layer_a_body,
        grid=(B, S // T),
        in_specs=[
            tok, layer(4, D_MODEL),
            layer(D_MODEL, D_MODEL), layer(MAIN_W, MAIN_W), layer(1, MAIN_W),
            _kvm_spec(l),
            layer(D_MODEL, D_MODEL), layer(D_MODEL, 2 * D_FF), layer(D_FF, D_MODEL),
        ],
        out_specs=tok,
        out_shape=jax.ShapeDtypeStruct(x.shape, F32),
        scratch_shapes=[pltpu.VMEM((T + POOL_HALO, MAIN_W), F32)],
        compiler_params=_params(2),
        name="layer_a",
    )(x, gains, w_in, w_pool_bd, pool_scale, kvm, w_out, w_gu, w_down)


def _dil_shape(B, S, g):
    d = DIL_PATTERNS[g][1]
    return (B, d, S // d, GROUP_W)


def _dil_spec(g, T):
    d = DIL_PATTERNS[g][1]
    return pl.BlockSpec((None, d, T // d, GROUP_W), lambda b, j: (b, 0, j, 0))


def _scatter_by_residue(val, slabs, out_ref, g, lo):
    d = DIL_PATTERNS[g][1]
    H = val.shape[0]
    if d == 1:
        out_ref[0, lo:lo + H, :] = val.astype(BF16)
        return
    for p in range(GROUP_W // LANES):
        slabs[p] = val[:, p * LANES:(p + 1) * LANES]
    for r in range(d):
        for p in range(GROUP_W // LANES):
            out_ref[r, lo // d:(lo + H) // d, p * LANES:(p + 1) * LANES] = (
                slabs[p, pl.ds(r, H // d, stride=d), :].astype(BF16))


def _gather_by_residue(in_ref, slabs, g, lo, H):
    d = DIL_PATTERNS[g][1]
    if d == 1:
        return in_ref[0, lo:lo + H, :].astype(F32)
    for r in range(d):
        for p in range(GROUP_W // LANES):
            slabs[p, pl.ds(r, H // d, stride=d), :] = (
                in_ref[r, lo // d:(lo + H) // d, p * LANES:(p + 1) * LANES].astype(F32))
    return jnp.concatenate([slabs[p] for p in range(GROUP_W // LANES)], axis=1)


def _slab_scratch(n_chains, n, chain_rows):
    return pltpu.VMEM((n_chains, n, GROUP_W // LANES, chain_rows, LANES), F32)


def _mix_b_in_rows(z, lo, cos_ref, sin_ref, kvm_ref, q_refs, ymem_ref, slabs):
    rows = slice(lo, lo + z.shape[0])
    ymem_ref[rows, :] = _memory_attention(z[:, MAIN_W:], kvm_ref[...]).astype(BF16)
    q = _rope(z[:, :MAIN_W], cos_ref[rows, :], sin_ref[rows, :]) * Q_SCALE
    for g, q_ref in enumerate(q_refs):
        _scatter_by_residue(q[:, g * GROUP_W:(g + 1) * GROUP_W], slabs, q_ref, g, lo)


def _mix_b_in_body(*refs, with_kv):
    x_ref, g0_ref, w_in_ref, cos_ref, sin_ref, kvm_ref = refs[:6]
    n_in = 8 if with_kv else 6
    q_refs, ymem_ref, slabs = refs[n_in:n_in + 3], refs[n_in + 3], refs[-1]
    chains = range(x_ref.shape[0] // SUB_TILE)
    rows = [slice(i * SUB_TILE, (i + 1) * SUB_TILE) for i in chains]
    xn = []
    for i in chains:
        x = x_ref[rows[i], :]
        xn.append(x * lax.rsqrt(jnp.mean(x * x, axis=-1, keepdims=True) + EPS))
    z = [jnp.dot((xn[i] * g0_ref[...]).astype(BF16), w_in_ref[...], preferred_element_type=F32) for i in chains]
    if with_kv:
        kv_norm_ref, w_kv_ref = refs[6:8]
        k_refs, v_refs = refs[n_in + 4:n_in + 7], refs[n_in + 7:n_in + 10]
        kvs = [jnp.dot((xn[i] * kv_norm_ref[...]).astype(BF16), w_kv_ref[...], preferred_element_type=F32)
               for i in chains]
    for i in chains:
        _mix_b_in_rows(z[i], i * SUB_TILE, cos_ref, sin_ref, kvm_ref, q_refs, ymem_ref, slabs.at[i, 0])
    if with_kv:
        for i in chains:
            kv, lo = kvs[i], i * SUB_TILE
            k = _rope(kv[:, :MAIN_W], cos_ref[rows[i], :], sin_ref[rows[i], :])
            for g in range(N_GROUPS):
                _scatter_by_residue(k[:, g * GROUP_W:(g + 1) * GROUP_W], slabs.at[i, 1], k_refs[g], g, lo)
                _scatter_by_residue(kv[:, MAIN_W + g * GROUP_W:MAIN_W + (g + 1) * GROUP_W], slabs.at[i, 2],
                                    v_refs[g], g, lo)


def _mix_b_in(x, g0, w_in, cos_t, sin_t, kvm, l, kv=None):
    B, S, _ = x.shape
    T = MIX_TILE
    tab = pl.BlockSpec((None, T, LANES), lambda b, j: (b, j, 0))
    vec = pl.BlockSpec((1, D_MODEL), lambda b, j: (0, 0))
    grp_specs = [_dil_spec(g, T) for g in range(N_GROUPS)]
    grp_shapes = [jax.ShapeDtypeStruct(_dil_shape(B, S, g), BF16) for g in range(N_GROUPS)]
    in_specs = [
        pl.BlockSpec((None, T, D_MODEL), lambda b, j: (b, j, 0)), vec,
        pl.BlockSpec((None, D_MODEL, D_MODEL), lambda b, j: (l, 0, 0)),
        tab, tab,
        _kvm_spec(l),
    ]
    args = [x, g0, w_in, cos_t, sin_t, kvm]
    out_specs = grp_specs + [pl.BlockSpec((None, T, MEM_W), lambda b, j: (b, j, 0))]
    out_shape = grp_shapes + [jax.ShapeDtypeStruct((B, S, MEM_W), BF16)]
    if kv is not None:
        in_specs += [vec, pl.BlockSpec((D_MODEL, 2 * MAIN_W), lambda b, j: (0, 0))]
        args += list(kv)
        out_specs, out_shape = out_specs + grp_specs * 2, out_shape + grp_shapes * 2
    res = pl.pallas_call(
        functools.partial(_mix_b_in_body, with_kv=kv is not None),
        grid=(B, S // T),
        in_specs=in_specs,
        out_specs=out_specs,
        out_shape=out_shape,
        scratch_shapes=[_slab_scratch(T // SUB_TILE, 3 if kv is not None else 1, SUB_TILE)],
        compiler_params=_params(2),
        name="mix_b_in",
    )(*args)
    n = N_GROUPS
    q_ymem = (res[:n], res[n])
    return (q_ymem, (res[n + 1:2 * n + 1], res[2 * n + 1:])) if kv is not None else q_ymem


def _band_attn_body(q_ref, k_ref, v_ref, o_ref, lse_ref):
    A = ATTN_BLOCK
    n_seq, L, _ = q_ref.shape
    qi = lax.broadcasted_iota(jnp.int32, (2 * A, 2 * A), 0) & (A - 1)
    kj = lax.broadcasted_iota(jnp.int32, (2 * A, 2 * A), 1)
    band = (kj >= qi) & (kj <= qi + A)
    causal = (lax.broadcasted_iota(jnp.int32, (2 * A, A), 1)
              <= (lax.broadcasted_iota(jnp.int32, (2 * A, A), 0) & (A - 1)))
    qi2 = lax.broadcasted_iota(jnp.int32, (4 * A, 2 * A), 0)
    kj2 = lax.broadcasted_iota(jnp.int32, (4 * A, 2 * A), 1)
    causal2 = ((qi2 >= 2 * A) == (kj2 >= A)) & ((kj2 & (A - 1)) <= (qi2 & (A - 1)))
    band, causal, causal2 = [jnp.where(m, 0.0, NEG).astype(F32) for m in (band, causal, causal2)]

    first = [(r, 0) for r in range(n_seq)]
    units = [(first[i:i + 2], causal2) for i in range(0, n_seq - 1, 2)]
    if n_seq % 2:
        units.append((first[-1:], causal))
    units += [([(r, n)], band) for r in range(n_seq) for n in range(1, L // A)]

    krows = lambda n: slice(0, A) if n == 0 else slice((n - 1) * A, (n + 1) * A)
    cat = lambda parts: parts[0] if len(parts) == 1 else jnp.concatenate(parts, axis=0)
    for members, mask in units:
        for p in range(GROUP_W // LANES):
            cols = slice(p * LANES, (p + 1) * LANES)
            q = cat([_split_heads(q_ref[r, n * A:(n + 1) * A, cols]) for r, n in members])
            k = cat([k_ref[r, krows(n), cols] for r, n in members])
            v = cat([v_ref[r, krows(n), cols] for r, n in members])
            s = lax.dot_general(q, k, (((1,), (1,)), ((), ())), preferred_element_type=F32) + mask
            m = jnp.max(s, axis=-1, keepdims=True)
            e = jnp.exp2(s - m).astype(BF16)
            oe = jnp.dot(e, _with_ones(v), preferred_element_type=F32)
            for i, (r, n) in enumerate(members):
                part = slice(i * 2 * A, (i + 1) * 2 * A)
                den = _merge_heads(oe[part, LANES:])
                o_ref[r, n * A:(n + 1) * A, cols] = (_merge_heads(oe[part, :LANES]) / den).astype(BF16)
                lse_ref[r, n * A:(n + 1) * A, cols] = (
                    _merge_heads(jnp.broadcast_to(m[part], (2 * A, LANES))) + jnp.log2(den))


def _band_attn(q, k, v, g):
    B, d, L, _ = q.shape
    assert DIL_PATTERNS[g][0] // d == ATTN_BLOCK and L % ATTN_BLOCK == 0
    spec = pl.BlockSpec((None, d, L, GROUP_W), lambda b: (b, 0, 0, 0))
    return pl.pallas_call(
        _band_attn_body,
        grid=(B,),
        in_specs=[spec, spec, spec],
        out_specs=[spec, spec],
        out_shape=[jax.ShapeDtypeStruct(q.shape, BF16), jax.ShapeDtypeStruct(q.shape, F32)],
        compiler_params=_params(1),
        name=f"band_attn_g{g}",
    )(q, k, v)


def _mix_b_out_rows(x, lo, g1, o_refs, l_refs, ymem_ref, w_out_ref, slabs):
    H = x.shape[0]
    outs = [_gather_by_residue(r, slabs.at[2 * g], g, lo, H) for g, r in enumerate(o_refs)]
    lses = [_gather_by_residue(r, slabs.at[2 * g + 1], g, lo, H) for g, r in enumerate(l_refs)]
    m = jnp.maximum(jnp.maximum(lses[0], lses[1]), lses[2])
    es = [jnp.exp2(l - m) for l in lses]
    inv = 1.0 / (es[0] + es[1] + es[2])
    cols = [(outs[g] * (es[g] * inv)).astype(BF16) for g in range(N_GROUPS)]
    y = jnp.concatenate(cols + [ymem_ref[lo:lo + H, :]], axis=1)
    y = jnp.dot(y, w_out_ref[...], preferred_element_type=F32)
    return x + _rms(y, g1)


def _layer_b_body(x_ref, g_ref, o0_ref, o1_ref, o2_ref, l0_ref, l1_ref, l2_ref, ymem_ref,
                  w_out_ref, w_gu_ref, w_down_ref, out_ref, slabs):
    _, g1, g2, g3 = _gain_rows(g_ref)
    chains = range(x_ref.shape[0] // B_CHAIN)
    x1 = [_mix_b_out_rows(x_ref[i * B_CHAIN:(i + 1) * B_CHAIN, :], i * B_CHAIN, g1, (o0_ref, o1_ref, o2_ref),
                          (l0_ref, l1_ref, l2_ref), ymem_ref, w_out_ref, slabs.at[i]) for i in chains]
    for i in chains:
        out_ref[i * B_CHAIN:(i + 1) * B_CHAIN, :] = _ffn_rows(x1[i], g2, g3, w_gu_ref, w_down_ref)


def _layer_b_tail(x, gains, outs, lses, ymem, w_out, w_gu, w_down, l):
    B, S, _ = x.shape
    T = B_TILE
    tok = pl.BlockSpec((None, T, D_MODEL), lambda b, j: (b, j, 0))
    grp = [_dil_spec(g, T) for g in range(N_GROUPS)]
    layer = lambda *shape: _resident((None,) + shape, lambda b, j: (l,) + (0,) * len(shape))
    return pl.pallas_call(
        _layer_b_body,
        grid=(B, S // T),
        in_specs=[
            tok, layer(4, D_MODEL),
            *grp, *grp,
            pl.BlockSpec((None, T, MEM_W), lambda b, j: (b, j, 0)),
            layer(D_MODEL, D_MODEL), layer(D_MODEL, 2 * D_FF), layer(D_FF, D_MODEL),
        ],
        out_specs=tok,
        out_shape=jax.ShapeDtypeStruct(x.shape, F32),
        scratch_shapes=[_slab_scratch(T // B_CHAIN, 2 * N_GROUPS, B_CHAIN)],
        compiler_params=_params(2),
        name="layer_b_tail",
    )(x, gains, *outs, *lses, ymem, w_out, w_gu, w_down)


def _pool_block_diag(w_pool):
    n_layers, n_grp = w_pool.shape[:2]
    out = jnp.zeros((n_layers, MAIN_W, MAIN_W), w_pool.dtype)
    for g in range(n_grp):
        sl = slice(g * POOL_GROUP, (g + 1) * POOL_GROUP)
        out = out.at[:, sl, sl].set(w_pool[:, g])
    return out


@jax.jit
def kernel(x, mem, positions, norm_gains, mem_norm, w_in, w_mem_kv, w_out, w_pool, pool_scale,
           kv_norm, w_kv, w_gate_up, w_down):
    w_in_b = w_in.astype(BF16)
    w_out_b = w_out.astype(BF16)
    w_gu_b = w_gate_up.astype(BF16)
    w_down_b = w_down.astype(BF16)
    w_pool_b = _pool_block_diag(w_pool).astype(BF16)
    pool_scale = pool_scale.reshape(N_A_LAYERS, 1, MAIN_W)

    kvm = _mem_kv(mem, mem_norm, w_mem_kv.astype(BF16))
    cos_t, sin_t = _rope_tables(positions)
    kg = vg = None
    for l in range(DEPTH):
        if l < N_A_LAYERS:
            x = _layer_a(x, norm_gains, w_in_b, w_pool_b, pool_scale, kvm, w_out_b, w_gu_b, w_down_b, l)
            continue
        g0 = norm_gains[l, 0].reshape(1, D_MODEL)
        if l == N_A_LAYERS:
            (qg, ymem), (kg, vg) = _mix_b_in(x, g0, w_in_b, cos_t, sin_t, kvm, l,
                                             kv=(kv_norm.reshape(1, D_MODEL), w_kv.astype(BF16)))
        else:
            qg, ymem = _mix_b_in(x, g0, w_in_b, cos_t, sin_t, kvm, l)
        res = [_band_attn(qg[g], kg[g], vg[g], g) for g in range(N_GROUPS)]
        x = _layer_b_tail(x, norm_gains, [r[0] for r in res], [r[1] for r in res], ymem,
                          w_out_b, w_gu_b, w_down_b, l)
    return x
```
